```python
import jax, jax.numpy as jnp
from jax import lax
import numpy as np

D_MODEL = 1024
BATCH = 4
SEQ = 8192
DEPTH = 2

ROPE_THETA = 500000.0
NORM_EPS = 1e-6
Q_BLOCK = 128
NEG_INF = -1e30

MLA_HEADS = 8
MLA_Q_RANK = 256
MLA_KV_RANK = 128
MLA_NOPE_DIM = 64
MLA_ROPE_DIM = 32
MLA_V_DIM = 64
MLA_QK_DIM = MLA_NOPE_DIM + MLA_ROPE_DIM

NSA_HEADS = 8
NSA_KV_HEADS = 2
NSA_GROUP = NSA_HEADS // NSA_KV_HEADS
NSA_HEAD_DIM = 64
NSA_ROT_DIM = NSA_HEAD_DIM // 4
NSA_BRANCHES = 3
CMP_LEN = 32
CMP_STRIDE = 16
CMP_HIDDEN = 128
SLC_LEN = 64
SLC_TOPK = 16
WINDOW = 512
FORCE_SCORE = 1e6

D_MIX = MLA_HEADS * MLA_V_DIM + NSA_HEADS * NSA_HEAD_DIM
D_FF = 2816
CONV_WIDTH = 3

IN_SPLITS = (MLA_Q_RANK, MLA_KV_RANK, MLA_ROPE_DIM, NSA_HEADS * NSA_HEAD_DIM,
             NSA_BRANCHES * 2 * NSA_KV_HEADS * NSA_HEAD_DIM, NSA_HEADS * NSA_BRANCHES)
IN_COLS = sum(IN_SPLITS)
SPLIT_POINTS = tuple(int(v) for v in np.cumsum(IN_SPLITS)[:-1])

kernel_name = 'hybrid_mla_nsa_convffn'


def rms_norm(x, g):
    xf = x.astype(jnp.float32)
    y = xf * lax.rsqrt(jnp.mean(xf * xf, axis=-1, keepdims=True) + NORM_EPS)
    return (y * g.astype(jnp.float32)).astype(x.dtype)


def rope_tables(seq, dim):
    inv = ROPE_THETA ** (-jnp.arange(0, dim, 2, dtype=jnp.float32) / dim)
    ang = jnp.arange(seq, dtype=jnp.float32)[:, None] * inv[None, :]
    return jnp.cos(ang), jnp.sin(ang)


def apply_rope(x, cos, sin):
    half = x.shape[-1] // 2
    x1, x2 = x[..., :half], x[..., half:]
    c = cos[None, :, None, :].astype(x.dtype)
    s = sin[None, :, None, :].astype(x.dtype)
    return jnp.concatenate([x1 * c - x2 * s, x1 * s + x2 * c], axis=-1)


def partial_rope(x, cos, sin):
    return jnp.concatenate([apply_rope(x[..., :NSA_ROT_DIM], cos, sin), x[..., NSA_ROT_DIM:]], axis=-1)


def masked_softmax(logits, mask):
    p = jax.nn.softmax(jnp.where(mask, logits.astype(jnp.float32), NEG_INF), axis=-1)
    return jnp.where(mask, p, 0.0)


def to_blocks(t):
    b, s = t.shape[:2]
    t = t.reshape((b, s // Q_BLOCK, Q_BLOCK) + t.shape[2:])
    return jnp.moveaxis(t, 1, 0)


def from_blocks(t):
    t = jnp.moveaxis(t, 0, 1)
    return t.reshape((t.shape[0], t.shape[1] * t.shape[2]) + t.shape[3:])


def mla_attention(q_lat, kv_lat, k_rope, q_norm, kv_norm, w_uq, w_ukv, cos, sin):
    b, s, _ = q_lat.shape
    q = (rms_norm(q_lat, q_norm) @ w_uq).reshape(b, s, MLA_HEADS, MLA_QK_DIM)
    q = jnp.concatenate([q[..., :MLA_NOPE_DIM], apply_rope(q[..., MLA_NOPE_DIM:], cos, sin)], axis=-1)
    kv = (rms_norm(kv_lat, kv_norm) @ w_ukv).reshape(b, s, MLA_HEADS, MLA_NOPE_DIM + MLA_V_DIM)
    k_pe = apply_rope(k_rope[:, :, None, :], cos, sin)
    k = jnp.concatenate([kv[..., :MLA_NOPE_DIM],
                         jnp.broadcast_to(k_pe, (b, s, MLA_HEADS, MLA_ROPE_DIM))], axis=-1)
    v = kv[..., MLA_NOPE_DIM:]
    scale = MLA_QK_DIM ** -0.5
    kpos = jnp.arange(s)

    def block(args):
        i, qi = args
        qpos = i * Q_BLOCK + jnp.arange(Q_BLOCK)
        logits = jnp.einsum('bqhd,bkhd->bhqk', qi, k) * scale
        p = masked_softmax(logits, kpos[None, :] <= qpos[:, None])
        return jnp.einsum('bhqk,bkhd->bqhd', p.astype(v.dtype), v)

    o = lax.map(block, (jnp.arange(s // Q_BLOCK), to_blocks(q)))
    return from_blocks(o).reshape(b, s, MLA_HEADS * MLA_V_DIM)


def compress(t, pos, w1, b1, w2):
    b, s, g, d = t.shape
    n_c = (s - CMP_LEN) // CMP_STRIDE + 1
    idx = jnp.arange(n_c)[:, None] * CMP_STRIDE + jnp.arange(CMP_LEN)[None, :]
    blocks = t[:, idx] + pos[None, None, :, None, :]
    flat = jnp.moveaxis(blocks, 3, 2).reshape(b, n_c, g, CMP_LEN * d)
    return jax.nn.gelu(flat @ w1 + b1) @ w2


def nsa_attention(q, kv, gate_logits, gate_bias, cmp_pos, cmp_w1, cmp_b1, cmp_w2, cos, sin):
    b, s = q.shape[:2]
    q = partial_rope(q, cos, sin)
    k_cmp = compress(partial_rope(kv[:, :, 0, 0], cos, sin), cmp_pos[0], cmp_w1[0], cmp_b1[0], cmp_w2[0])
    v_cmp = compress(kv[:, :, 0, 1], cmp_pos[1], cmp_w1[1], cmp_b1[1], cmp_w2[1])
    k_slc = jnp.moveaxis(partial_rope(kv[:, :, 1, 0], cos, sin), 2, 1)
    v_slc = jnp.moveaxis(kv[:, :, 1, 1], 2, 1)
    pad = ((0, 0), (WINDOW, 0), (0, 0), (0, 0))
    k_win = jnp.pad(partial_rope(kv[:, :, 2, 0], cos, sin), pad)
    v_win = jnp.pad(kv[:, :, 2, 1], pad)
    gates = jax.nn.sigmoid((gate_logits + gate_bias).astype(jnp.float32)).astype(q.dtype)
    gates = gates.reshape(b, s, NSA_KV_HEADS, NSA_GROUP, NSA_BRANCHES)

    n_c = k_cmp.shape[1]
    n_s = s // SLC_LEN
    top = min(SLC_TOPK, n_s)
    cmp_start = jnp.arange(n_c) * CMP_STRIDE
    cmp_end = cmp_start + CMP_LEN - 1
    slc_start = jnp.arange(n_s) * SLC_LEN
    overlap = ((cmp_start[:, None] <= slc_start[None, :] + SLC_LEN - 1)
               & (cmp_end[:, None] >= slc_start[None, :])).astype(jnp.float32)
    blk_id = jnp.arange(n_s)
    tok_off = jnp.arange(SLC_LEN)
    b_ix = jnp.arange(b)[:, None, None]
    g_ix = jnp.arange(NSA_KV_HEADS)[None, :, None]
    scale = NSA_HEAD_DIM ** -0.5

    def block(args):
        i, qi, gi = args
        qpos = i * Q_BLOCK + jnp.arange(Q_BLOCK)
        qg = qi.reshape(b, Q_BLOCK, NSA_KV_HEADS, NSA_GROUP, NSA_HEAD_DIM)
        lc = jnp.einsum('bqghd,bcgd->bghqc', qg, k_cmp) * scale
        pc = masked_softmax(lc, cmp_end[None, :] <= qpos[:, None])
        o_cmp = jnp.einsum('bghqc,bcgd->bqghd', pc.astype(v_cmp.dtype), v_cmp)
        imp = jnp.einsum('bghqc,cs->bgqs', pc, overlap)
        cur = (qpos // SLC_LEN)[:, None]
        forced = (blk_id[None, :] == 0) | (blk_id[None, :] == cur) | (blk_id[None, :] == cur - 1)
        score = jnp.where(forced, FORCE_SCORE, jnp.where(blk_id[None, :] <= cur, imp, -1.0))
        top_val, top_idx = lax.top_k(score, top)
        tok = top_idx[..., None] * SLC_LEN + tok_off
        tok_mask = ((top_val >= 0.0)[..., None] & (tok <= qpos[:, None, None]))
        tok_mask = tok_mask.reshape(b, NSA_KV_HEADS, Q_BLOCK, top * SLC_LEN)
        flat_idx = tok.reshape(b, NSA_KV_HEADS, Q_BLOCK * top * SLC_LEN)
        k_sel = k_slc[b_ix, g_ix, flat_idx].reshape(b, NSA_KV_HEADS, Q_BLOCK, top * SLC_LEN, NSA_HEAD_DIM)
        v_sel = v_slc[b_ix, g_ix, flat_idx].reshape(b, NSA_KV_HEADS, Q_BLOCK, top * SLC_LEN, NSA_HEAD_DIM)
        ls = jnp.einsum('bqghd,bgqnd->bghqn', qg, k_sel) * scale
        ps = masked_softmax(ls, tok_mask[:, :, None])
        o_slc = jnp.einsum('bghqn,bgqnd->bqghd', ps.astype(v_sel.dtype), v_sel)
        kw = lax.dynamic_slice_in_dim(k_win, i * Q_BLOCK, Q_BLOCK + WINDOW, axis=1)
        vw = lax.dynamic_slice_in_dim(v_win, i * Q_BLOCK, Q_BLOCK + WINDOW, axis=1)
        kpos = i * Q_BLOCK - WINDOW + jnp.arange(Q_BLOCK + WINDOW)
        mw = ((kpos[None, :] >= 0) & (kpos[None, :] <= qpos[:, None])
              & (kpos[None, :] > qpos[:, None] - WINDOW))
        lw = jnp.einsum('bqghd,bkgd->bghqk', qg, kw) * scale
        pw = masked_softmax(lw, mw)
        o_win = jnp.einsum('bghqk,bkgd->bqghd', pw.astype(vw.dtype), vw)
        o = gi[..., 0:1] * o_cmp + gi[..., 1:2] * o_slc + gi[..., 2:3] * o_win
        return o.reshape(b, Q_BLOCK, NSA_HEADS * NSA_HEAD_DIM)

    o = lax.map(block, (jnp.arange(s // Q_BLOCK), to_blocks(q), to_blocks(gates)))
    return from_blocks(o)


def conv_ffn(x, w_up, conv_w, conv_b, w_down):
    h = x @ w_up
    c = h.shape[-1]
    h = lax.conv_general_dilated(h, conv_w[:, None, :].astype(h.dtype), window_strides=(1,),
                                 padding=((CONV_WIDTH - 1, 0),),
                                 dimension_numbers=('NWC', 'WIO', 'NWC'),
                                 feature_group_count=c) + conv_b
    u, v = jnp.split(h, 2, axis=-1)
    return (jax.nn.silu(u) * v) @ w_down


def setup_inputs(seed: int = 0) -> dict:
    key = jax.random.key(seed)
    ks = jax.random.split(key, 20)

    def nrm(k, shape, scale):
        return jax.random.normal(k, shape, jnp.float32) * scale

    return {
        'x': nrm(ks[0], (BATCH, SEQ, D_MODEL), 1.0),
        'attn_norm': 1.0 + nrm(ks[1], (DEPTH, D_MODEL), 0.02),
        'w_in': nrm(ks[2], (DEPTH, D_MODEL, IN_COLS), D_MODEL ** -0.5),
        'mla_q_norm': 1.0 + nrm(ks[3], (DEPTH, MLA_Q_RANK), 0.02),
        'mla_kv_norm': 1.0 + nrm(ks[4], (DEPTH, MLA_KV_RANK), 0.02),
        'mla_w_uq': nrm(ks[5], (DEPTH, MLA_Q_RANK, MLA_HEADS * MLA_QK_DIM), MLA_Q_RANK ** -0.5),
        'mla_w_ukv': nrm(ks[6], (DEPTH, MLA_KV_RANK, MLA_HEADS * (MLA_NOPE_DIM + MLA_V_DIM)), MLA_KV_RANK ** -0.5),
        'nsa_gate_bias': nrm(ks[7], (DEPTH, NSA_HEADS * NSA_BRANCHES), 0.02),
        'nsa_cmp_pos': nrm(ks[8], (DEPTH, 2, CMP_LEN, NSA_HEAD_DIM), 0.1),
        'nsa_cmp_w1': nrm(ks[9], (DEPTH, 2, CMP_LEN * NSA_HEAD_DIM, CMP_HIDDEN), (CMP_LEN * NSA_HEAD_DIM) ** -0.5),
        'nsa_cmp_b1': nrm(ks[10], (DEPTH, 2, CMP_HIDDEN), 0.02),
        'nsa_cmp_w2': nrm(ks[11], (DEPTH, 2, CMP_HIDDEN, NSA_HEAD_DIM), CMP_HIDDEN ** -0.5),
        'w_out': nrm(ks[12], (DEPTH, D_MIX, D_MODEL), D_MIX ** -0.5),
        'ffn_norm': 1.0 + nrm(ks[13], (DEPTH, D_MODEL), 0.02),
        'ffn_w_up': nrm(ks[14], (DEPTH, D_MODEL, 2 * D_FF), D_MODEL ** -0.5),
        'ffn_conv_w': nrm(ks[15], (DEPTH, CONV_WIDTH, 2 * D_FF), CONV_WIDTH ** -0.5),
        'ffn_conv_b': nrm(ks[16], (DEPTH, 2 * D_FF), 0.02),
        'ffn_w_down': nrm(ks[17], (DEPTH, D_FF, D_MODEL), D_FF ** -0.5),
        'final_norm': 1.0 + nrm(ks[18], (D_MODEL,), 0.02),
    }


def reference(x, attn_norm, w_in, mla_q_norm, mla_kv_norm, mla_w_uq, mla_w_ukv, nsa_gate_bias,
              nsa_cmp_pos, nsa_cmp_w1, nsa_cmp_b1, nsa_cmp_w2, w_out, ffn_norm, ffn_w_up,
              ffn_conv_w, ffn_conv_b, ffn_w_down, final_norm):
    b, s, _ = x.shape
    cos_mla, sin_mla = rope_tables(s, MLA_ROPE_DIM)
    cos_nsa, sin_nsa = rope_tables(s, NSA_ROT_DIM)
    for l in range(DEPTH):
        h = rms_norm(x, attn_norm[l])
        proj = h @ w_in[l]
        q_lat, kv_lat, k_rope, nsa_q, nsa_kv, gate_logits = jnp.split(proj, SPLIT_POINTS, axis=-1)
        o_mla = mla_attention(q_lat, kv_lat, k_rope, mla_q_norm[l], mla_kv_norm[l],
                              mla_w_uq[l], mla_w_ukv[l], cos_mla, sin_mla)
        o_nsa = nsa_attention(nsa_q.reshape(b, s, NSA_HEADS, NSA_HEAD_DIM),
                              nsa_kv.reshape(b, s, NSA_BRANCHES, 2, NSA_KV_HEADS, NSA_HEAD_DIM),
                              gate_logits, nsa_gate_bias[l], nsa_cmp_pos[l], nsa_cmp_w1[l],
                              nsa_cmp_b1[l], nsa_cmp_w2[l], cos_nsa, sin_nsa)
        x = x + jnp.concatenate([o_mla, o_nsa], axis=-1) @ w_out[l]
        h = rms_norm(x, ffn_norm[l])
        x = x + conv_ffn(h, ffn_w_up[l], ffn_conv_w[l], ffn_conv_b[l], ffn_w_down[l])
    return rms_norm(x, final_norm)
```

```python
import functools

import jax
import jax.numpy as jnp
import numpy as np
from jax import lax
from jax.experimental import pallas as pl
from jax.experimental.pallas import tpu as pltpu

F32 = jnp.float32
BF16 = jnp.bfloat16

LANES = 128

D_MODEL = 1024
ROPE_THETA = 500000.0
NORM_EPS = 1e-6
NEG_INF = -1e30

MLA_HEADS = 8
MLA_Q_RANK = 256
MLA_KV_RANK = 128
MLA_NOPE_DIM = 64
MLA_ROPE_DIM = 32
MLA_V_DIM = 64
MLA_QK_DIM = MLA_NOPE_DIM + MLA_ROPE_DIM

NSA_HEADS = 8
NSA_KV_HEADS = 2
NSA_GROUP = NSA_HEADS // NSA_KV_HEADS
NSA_HEAD_DIM = 64
NSA_ROT_DIM = NSA_HEAD_DIM // 4
NSA_BRANCHES = 3
CMP_LEN = 32
CMP_STRIDE = 16
CMP_HIDDEN = 128
SLC_LEN = 64
SLC_TOPK = 16
WINDOW = 512
FORCE_SCORE = 1e6

D_FF = 2816
CONV_WIDTH = 3

IN_SPLITS = (MLA_Q_RANK, MLA_KV_RANK, MLA_ROPE_DIM, NSA_HEADS * NSA_HEAD_DIM,
             NSA_BRANCHES * 2 * NSA_KV_HEADS * NSA_HEAD_DIM, NSA_HEADS * NSA_BRANCHES)
_OFF = tuple(int(v) for v in np.cumsum((0,) + IN_SPLITS))

C_QLAT = 0
C_KVLAT = C_QLAT + MLA_Q_RANK
C_KPE = C_KVLAT + MLA_KV_RANK
C_NQ = C_KPE + LANES
C_NKV = C_NQ + NSA_HEADS * NSA_HEAD_DIM
C_GATE = C_NKV + NSA_BRANCHES * NSA_KV_HEADS * LANES
C_END = C_GATE + NSA_KV_HEADS * LANES

VMEM_LIMIT = 56 * 1024 * 1024

TM_PROJ = 512
TQ_MLA = 512
TQ_NSA = 256
TK_SLC = 512
TM_FFN = 512
TF_FFN = 1408
HALO = 16


def _rms(x, g):
    return x * lax.rsqrt(jnp.mean(x * x, axis=-1, keepdims=True) + NORM_EPS) * g


def _rope(x, tab_ref, half):
    return (x * tab_ref[0] + pltpu.roll(x, LANES - half, 1) * tab_ref[1]
            + pltpu.roll(x, half, 1) * tab_ref[2])


def _dot(a, b):
    return jnp.dot(a, b, preferred_element_type=F32)


def _dot_nt(a, b):
    return lax.dot_general(a, b, (((1,), (1,)), ((), ())), preferred_element_type=F32)


def _proj_kernel(x_ref, g_ref, win_ref, qn_ref, kvn_ref, wuq_ref, wukv_ref, gb_ref,
                 tmla_ref, tnq_ref, tnk_ref,
                 mq_ref, mkv_ref, nq_ref, nkv_ref, ncmp_ref, gate_ref):
    h = _rms(x_ref[0], g_ref[...])
    p = _dot(h.astype(BF16), win_ref[...])

    ql = _rms(p[:, C_QLAT:C_QLAT + MLA_Q_RANK], qn_ref[...])
    q = _dot(ql.astype(BF16), wuq_ref[...])
    scale = MLA_QK_DIM ** -0.5
    for hh in range(MLA_HEADS):
        blk = q[:, hh * LANES:(hh + 1) * LANES]
        mq_ref[0, :, hh * LANES:(hh + 1) * LANES] = (
            _rope(blk, tmla_ref, MLA_ROPE_DIM // 2) * scale).astype(BF16)

    kpe = _rope(p[:, C_KPE:C_KPE + LANES], tmla_ref, MLA_ROPE_DIM // 2)
    kvl = _rms(p[:, C_KVLAT:C_KVLAT + MLA_KV_RANK], kvn_ref[...])
    kv = _dot(kvl.astype(BF16), wukv_ref[...])
    for hh in range(MLA_HEADS):
        c0 = hh * 2 * LANES
        mkv_ref[0, :, c0:c0 + LANES] = (kv[:, c0:c0 + LANES] + kpe).astype(BF16)
        mkv_ref[0, :, c0 + LANES:c0 + 2 * LANES] = kv[:, c0 + LANES:c0 + 2 * LANES].astype(BF16)

    nscale = NSA_HEAD_DIM ** -0.5
    for j in range(NSA_HEADS * NSA_HEAD_DIM // LANES):
        blk = p[:, C_NQ + j * LANES:C_NQ + (j + 1) * LANES]
        nq_ref[0, :, j * LANES:(j + 1) * LANES] = (
            _rope(blk, tnq_ref, NSA_ROT_DIM // 2) * nscale).astype(BF16)

    for j in range(NSA_BRANCHES * NSA_KV_HEADS):
        blk = _rope(p[:, C_NKV + j * LANES:C_NKV + (j + 1) * LANES], tnk_ref, NSA_ROT_DIM // 2)
        if j < NSA_KV_HEADS:
            ncmp_ref[0, :, j * LANES:(j + 1) * LANES] = blk
        else:
            jj = j - NSA_KV_HEADS
            nkv_ref[0, :, jj * LANES:(jj + 1) * LANES] = blk.astype(BF16)

    gate_ref[0] = jax.nn.sigmoid(p[:, C_GATE:C_END] + gb_ref[...])


def _proj_call(x, w, tabs):
    b, s, d = x.shape
    tm = TM_PROJ
    const = lambda shape: pl.BlockSpec(shape, lambda i, bb: (0,) * len(shape))
    tab = pl.BlockSpec((3, tm, LANES), lambda i, bb: (0, i, 0))
    row = lambda width: pl.BlockSpec((1, tm, width), lambda i, bb: (bb, i, 0))
    return pl.pallas_call(
        _proj_kernel,
        grid=(s // tm, b),
        in_specs=[row(d), const((1, d)), const((d, C_END)), const((1, MLA_Q_RANK)),
                  const((1, MLA_KV_RANK)), const((MLA_Q_RANK, MLA_HEADS * LANES)),
                  const((MLA_KV_RANK, MLA_HEADS * 2 * LANES)), const((1, NSA_KV_HEADS * LANES)),
                  tab, tab, tab],
        out_specs=[row(MLA_HEADS * LANES), row(MLA_HEADS * 2 * LANES), row(NSA_HEADS * NSA_HEAD_DIM),
                   row(2 * NSA_KV_HEADS * LANES), row(NSA_KV_HEADS * LANES), row(NSA_KV_HEADS * LANES)],
        out_shape=[jax.ShapeDtypeStruct((b, s, MLA_HEADS * LANES), BF16),
                   jax.ShapeDtypeStruct((b, s, MLA_HEADS * 2 * LANES), BF16),
                   jax.ShapeDtypeStruct((b, s, NSA_HEADS * NSA_HEAD_DIM), BF16),
                   jax.ShapeDtypeStruct((b, s, 2 * NSA_KV_HEADS * LANES), BF16),
                   jax.ShapeDtypeStruct((b, s, NSA_KV_HEADS * LANES), F32),
                   jax.ShapeDtypeStruct((b, s, NSA_KV_HEADS * LANES), F32)],
        compiler_params=pltpu.CompilerParams(
            dimension_semantics=("arbitrary", "arbitrary"), vmem_limit_bytes=VMEM_LIMIT),
        name="proj",
    )(x, w["attn_norm"], w["w_in"], w["q_norm"], w["kv_norm"], w["w_uq"], w["w_ukv"], w["gate_bias"],
      tabs["mla"], tabs["nq"], tabs["nk"])


def _cmp_kernel(x_ref, pos_ref, w1_ref, b1_ref, w2_ref, o_ref):
    nc = o_ref.shape[2]
    half = CMP_LEN // 2
    a = jnp.zeros((nc, 2 * CMP_HIDDEN), F32)
    bm = jnp.zeros((nc, 2 * CMP_HIDDEN), F32)
    for l in range(half):
        xl = x_ref[0, pl.ds(l, nc, stride=CMP_STRIDE), :]
        a = a + _dot((xl + pos_ref[l:l + 1, :]).astype(BF16), w1_ref[l])
        bm = bm + _dot((xl + pos_ref[half + l:half + l + 1, :]).astype(BF16), w1_ref[half + l])
    hid = a + pltpu.roll(bm, nc - 1, 0) + b1_ref[...]
    o_ref[0, 0] = _dot(jax.nn.gelu(hid).astype(BF16), w2_ref[...]).astype(BF16)


def _cmp_call(ncmp, w):
    b, s, _ = ncmp.shape
    nc = s // CMP_STRIDE
    const = lambda shape: pl.BlockSpec(shape, lambda bb, g: (0,) * len(shape))
    return pl.pallas_call(
        _cmp_kernel,
        grid=(b, NSA_KV_HEADS),
        in_specs=[pl.BlockSpec((1, s, LANES), lambda bb, g: (bb, 0, g)),
                  const((CMP_LEN, LANES)), const((CMP_LEN, LANES, 2 * CMP_HIDDEN)),
                  const((1, 2 * CMP_HIDDEN)), const((2 * CMP_HIDDEN, LANES))],
        out_specs=pl.BlockSpec((1, 1, nc, LANES), lambda bb, g: (bb, g, 0, 0)),
        out_shape=jax.ShapeDtypeStruct((b, NSA_KV_HEADS, nc, LANES), BF16),
        compiler_params=pltpu.CompilerParams(
            dimension_semantics=("arbitrary", "arbitrary"), vmem_limit_bytes=VMEM_LIMIT),
        name="compress",
    )(ncmp, w["cmp_pos"], w["cmp_w1"], w["cmp_b1"], w["cmp_w2"])


def _mla_kernel(q_ref, kv_ref, o_ref, m_ref, l_ref, acc_ref):
    tq = q_ref.shape[1]
    tk = tq
    qi = pl.program_id(2)
    q = q_ref[0]
    m_ref[...] = jnp.full(m_ref.shape, NEG_INF, F32)
    l_ref[...] = jnp.zeros(l_ref.shape, F32)
    acc_ref[...] = jnp.zeros(acc_ref.shape, F32)

    def step(j, masked):
        k0 = pl.multiple_of(j * tk, tk)
        k = kv_ref[0, pl.ds(k0, tk), 0:LANES]
        v = kv_ref[0, pl.ds(k0, tk), LANES:2 * LANES]
        s = _dot_nt(q, k)
        if masked:
            s = jnp.where(lax.broadcasted_iota(jnp.int32, s.shape, 1)
                          <= lax.broadcasted_iota(jnp.int32, s.shape, 0), s, NEG_INF)
        m_old = m_ref[...]
        m_new = jnp.maximum(m_old, jnp.max(s, axis=1, keepdims=True))
        alpha = jnp.exp(m_old - m_new)
        p = jnp.exp(s - m_new)
        l_ref[...] = alpha * l_ref[...] + jnp.sum(p, axis=1, keepdims=True)
        acc_ref[...] = alpha * acc_ref[...] + _dot(p.astype(BF16), v)
        m_ref[...] = m_new

    def body(j, c):
        step(j, False)
        return c

    lax.fori_loop(0, qi, body, 0)
    step(qi, True)
    o_ref[0] = (acc_ref[...] / l_ref[...]).astype(BF16)


def _mla_call(mq, mkv):
    b, s, _ = mq.shape
    tq = TQ_MLA
    return pl.pallas_call(
        _mla_kernel,
        grid=(b, MLA_HEADS, s // tq),
        in_specs=[pl.BlockSpec((1, tq, LANES), lambda bb, h, i: (bb, i, h)),
                  pl.BlockSpec((1, s, 2 * LANES), lambda bb, h, i: (bb, 0, h))],
        out_specs=pl.BlockSpec((1, tq, LANES), lambda bb, h, i: (bb, i, h)),
        out_shape=jax.ShapeDtypeStruct((b, s, MLA_HEADS * LANES), BF16),
        scratch_shapes=[pltpu.VMEM((tq, 1), F32), pltpu.VMEM((tq, 1), F32), pltpu.VMEM((tq, LANES), F32)],
        compiler_params=pltpu.CompilerParams(
            dimension_semantics=("arbitrary", "arbitrary", "arbitrary"), vmem_limit_bytes=VMEM_LIMIT),
        name="mla_attn",
    )(mq, mkv)


def _nsa_kernel(q_ref, kvc_ref, kvs_ref, kvw_ref, gate_ref, ov_ref, ex_ref, o_ref,
                m_ref, l_ref, acc_ref, *, seq):
    tq = q_ref.shape[1]
    hg = NSA_GROUP
    nc = kvc_ref.shape[2]
    tk = TK_SLC
    q0 = pl.program_id(2) * tq

    xq = q_ref[0].astype(F32)
    lane = lax.broadcasted_iota(jnp.int32, (tq, LANES), 1)
    parts = []
    for j in range(hg * NSA_HEAD_DIM // LANES):
        a = xq[:, j * LANES:(j + 1) * LANES]
        parts.append(jnp.where(lane < NSA_HEAD_DIM, a, 0.0))
        parts.append(jnp.where(lane < NSA_HEAD_DIM, pltpu.roll(a, NSA_HEAD_DIM, 1), 0.0))
    qst = jnp.concatenate(parts, axis=0).astype(BF16)

    kvc = kvc_ref[0, 0]
    sc = _dot_nt(qst, kvc)
    qpos_c = q0 + lax.broadcasted_iota(jnp.int32, (tq, nc), 0)
    cend = lax.broadcasted_iota(jnp.int32, (tq, nc), 1) * CMP_STRIDE + (CMP_LEN - 1)
    mask_c = cend <= qpos_c
    pcs = jnp.zeros((tq, nc), F32)
    o_cmp = []
    for h in range(hg):
        sh = jnp.where(mask_c, sc[h * tq:(h + 1) * tq], NEG_INF)
        e = jnp.exp(sh - jnp.max(sh, axis=1, keepdims=True))
        pc = jnp.where(mask_c, e, 0.0) * (1.0 / jnp.sum(e, axis=1, keepdims=True))
        pcs = pcs + pc
        o_cmp.append(_dot(pc.astype(BF16), kvc))
    ov = ov_ref[...]
    p_hi = pcs.astype(BF16)
    r1 = pcs - p_hi.astype(F32)
    p_mid = r1.astype(BF16)
    p_lo = (r1 - p_mid.astype(F32)).astype(BF16)
    imp = _dot(p_hi, ov) + _dot(p_mid, ov) + _dot(p_lo, ov)

    blk = lax.broadcasted_iota(jnp.int32, (tq, LANES), 1)
    qrow = q0 + lax.broadcasted_iota(jnp.int32, (tq, LANES), 0)
    cur = qrow // SLC_LEN
    forced = (blk == 0) | (blk == cur) | (blk == cur - 1)
    allowed = blk <= cur
    score = jnp.where(forced, FORCE_SCORE, jnp.where(allowed, imp, -1.0))
    blk_f = blk.astype(F32)
    picked = jnp.zeros((tq, LANES), F32)
    work = score
    for _ in range(min(SLC_TOPK, seq // SLC_LEN)):
        mx = jnp.max(work, axis=1, keepdims=True)
        first = jnp.min(jnp.where(work == mx, blk_f, float(LANES)), axis=1, keepdims=True)
        hit = blk_f == first
        picked = jnp.where(hit, 1.0, picked)
        work = jnp.where(hit, -2.0, work)
    sel = jnp.where(allowed, picked, 0.0).astype(BF16)

    m_ref[...] = jnp.full(m_ref.shape, NEG_INF, F32)
    l_ref[...] = jnp.zeros(l_ref.shape, F32)
    acc_ref[...] = jnp.zeros(acc_ref.shape, F32)
    qpos_k = q0 + lax.broadcasted_iota(jnp.int32, (tq, tk), 0)

    def slc_body(j, c):
        k0 = pl.multiple_of(j * tk, tk)
        kv = kvs_ref[0, pl.ds(k0, tk), :]
        s = _dot_nt(qst, kv)
        e = _dot(sel, ex_ref[:, pl.ds(k0, tk)])
        kpos = k0 + lax.broadcasted_iota(jnp.int32, (tq, tk), 1)
        ok = jnp.where(kpos <= qpos_k, e, 0.0) > 0.5
        for h in range(hg):
            rows = slice(h * tq, (h + 1) * tq)
            sh = jnp.where(ok, s[rows], NEG_INF)
            m_old = m_ref[rows]
            m_new = jnp.maximum(m_old, jnp.max(sh, axis=1, keepdims=True))
            alpha = jnp.exp(m_old - m_new)
            p = jnp.exp(sh - m_new)
            l_ref[rows] = alpha * l_ref[rows] + jnp.sum(p, axis=1, keepdims=True)
            acc_ref[rows] = alpha * acc_ref[rows] + _dot(p.astype(BF16), kv)
            m_ref[rows] = m_new
        return c

    lax.fori_loop(0, (q0 + tq + tk - 1) // tk, slc_body, 0)

    wl = WINDOW + tq
    start = pl.multiple_of(jnp.maximum(q0 - WINDOW, 0), tq)
    kvw = kvw_ref[0, pl.ds(start, wl), :]
    sw = _dot_nt(qst, kvw)
    qpos_w = q0 + lax.broadcasted_iota(jnp.int32, (tq, wl), 0)
    kpos_w = start + lax.broadcasted_iota(jnp.int32, (tq, wl), 1)
    mask_w = jnp.where(kpos_w <= qpos_w, kpos_w, -WINDOW - seq) > qpos_w - WINDOW

    gates = gate_ref[0]
    for h in range(hg):
        rows = slice(h * tq, (h + 1) * tq)
        sh = jnp.where(mask_w, sw[rows], NEG_INF)
        e = jnp.exp(sh - jnp.max(sh, axis=1, keepdims=True))
        o_win = _dot(e.astype(BF16), kvw) / jnp.sum(e, axis=1, keepdims=True)
        o_slc = acc_ref[rows] / l_ref[rows]
        c = h * NSA_BRANCHES
        o = gates[:, c:c + 1] * o_cmp[h] + gates[:, c + 1:c + 2] * o_slc + gates[:, c + 2:c + 3] * o_win
        o_ref[0, :, h * LANES:(h + 1) * LANES] = o.astype(BF16)


def _nsa_call(nq, kvc, nkv, gates, ov, ex):
    b, s, _ = nq.shape
    tq = TQ_NSA
    hg = NSA_GROUP
    nc = kvc.shape[2]
    return pl.pallas_call(
        functools.partial(_nsa_kernel, seq=s),
        grid=(b, NSA_KV_HEADS, s // tq),
        in_specs=[pl.BlockSpec((1, tq, hg * NSA_HEAD_DIM), lambda bb, g, i: (bb, i, g)),
                  pl.BlockSpec((1, 1, nc, LANES), lambda bb, g, i: (bb, g, 0, 0)),
                  pl.BlockSpec((1, s, LANES), lambda bb, g, i: (bb, 0, g)),
                  pl.BlockSpec((1, s, LANES), lambda bb, g, i: (bb, 0, NSA_KV_HEADS + g)),
                  pl.BlockSpec((1, tq, LANES), lambda bb, g, i: (bb, i, g)),
                  pl.BlockSpec((nc, LANES), lambda bb, g, i: (0, 0)),
                  pl.BlockSpec((LANES, s), lambda bb, g, i: (0, 0))],
        out_specs=pl.BlockSpec((1, tq, hg * LANES), lambda bb, g, i: (bb, i, g)),
        out_shape=jax.ShapeDtypeStruct((b, s, NSA_HEADS * LANES), BF16),
        scratch_shapes=[pltpu.VMEM((hg * tq, 1), F32), pltpu.VMEM((hg * tq, 1), F32),
                        pltpu.VMEM((hg * tq, LANES), F32)],
        compiler_params=pltpu.CompilerParams(
            dimension_semantics=("arbitrary", "arbitrary", "arbitrary"), vmem_limit_bytes=VMEM_LIMIT),
        name="nsa_attn",
    )(nq, kvc, nkv, nkv, gates, ov, ex)


def _ffn_kernel(x_ref, xh_ref, am_ref, amh_ref, an_ref, anh_ref, wom_ref, won_ref, g_ref,
                wu_ref, wv_ref, cwu_ref, cwv_ref, cbu_ref, cbv_ref, wd_ref, fn_ref, o_ref,
                x1_ref, hn_ref, acc_ref, *, final):
    tm = x_ref.shape[1]
    i = pl.program_id(1)
    f = pl.program_id(2)

    @pl.when(f == 0)
    def _():
        def piece(xv, am, an):
            x1 = xv + _dot(am, wom_ref[...]) + _dot(an, won_ref[...])
            return x1, _rms(x1, g_ref[...]).astype(BF16)

        x1, hn = piece(x_ref[0], am_ref[0], an_ref[0])
        x1_ref[...] = x1
        hn_ref[HALO:, :] = hn
        _, hnh = piece(xh_ref[0], amh_ref[0], anh_ref[0])
        hn_ref[0:HALO, :] = jnp.where(i > 0, hnh, jnp.zeros_like(hnh))
        acc_ref[...] = jnp.zeros(acc_ref.shape, F32)

    hn = hn_ref[...]

    def conv(w_ref, cw_ref, cb_ref):
        hh = _dot(hn, w_ref[...])
        return (hh[HALO:] * cw_ref[2:3, :] + hh[HALO - 1:HALO - 1 + tm] * cw_ref[1:2, :]
                + hh[HALO - 2:HALO - 2 + tm] * cw_ref[0:1, :] + cb_ref[...])

    u = conv(wu_ref, cwu_ref, cbu_ref)
    v = conv(wv_ref, cwv_ref, cbv_ref)
    acc_ref[...] += _dot((u * jax.nn.sigmoid(u) * v).astype(BF16), wd_ref[...])

    @pl.when(f == pl.num_programs(2) - 1)
    def _():
        y = x1_ref[...] + acc_ref[...]
        if final:
            y = _rms(y, fn_ref[...])
        o_ref[0] = y


def _ffn_call(x, am, an, w, final_norm, final):
    b, s, d = x.shape
    tm, tf = TM_FFN, TF_FFN
    nf = D_FF // tf
    hb = tm // HALO
    const = lambda shape: pl.BlockSpec(shape, lambda bb, i, f: (0,) * len(shape))
    row = lambda width: pl.BlockSpec((1, tm, width), lambda bb, i, f: (bb, i, 0))
    halo = lambda width: pl.BlockSpec((1, HALO, width), lambda bb, i, f: (bb, jnp.maximum(i * hb - 1, 0), 0))
    na = am.shape[2]
    return pl.pallas_call(
        functools.partial(_ffn_kernel, final=final),
        grid=(b, s // tm, nf),
        in_specs=[row(d), halo(d), row(na), halo(na), row(na), halo(na),
                  const((na, d)), const((na, d)), const((1, d)),
                  pl.BlockSpec((d, tf), lambda bb, i, f: (0, f)),
                  pl.BlockSpec((d, tf), lambda bb, i, f: (0, nf + f)),
                  pl.BlockSpec((CONV_WIDTH, tf), lambda bb, i, f: (0, f)),
                  pl.BlockSpec((CONV_WIDTH, tf), lambda bb, i, f: (0, nf + f)),
                  pl.BlockSpec((1, tf), lambda bb, i, f: (0, f)),
                  pl.BlockSpec((1, tf), lambda bb, i, f: (0, nf + f)),
                  pl.BlockSpec((tf, d), lambda bb, i, f: (f, 0)),
                  const((1, d))],
        out_specs=row(d),
        out_shape=jax.ShapeDtypeStruct((b, s, d), F32),
        scratch_shapes=[pltpu.VMEM((tm, d), F32), pltpu.VMEM((HALO + tm, d), BF16), pltpu.VMEM((tm, d), F32)],
        compiler_params=pltpu.CompilerParams(
            dimension_semantics=("arbitrary", "arbitrary", "arbitrary"), vmem_limit_bytes=VMEM_LIMIT),
        name="out_ffn",
    )(x, x, am, am, an, an, w["wo_m"], w["wo_n"], w["ffn_norm"], w["w_up"], w["w_up"],
      w["conv_w"], w["conv_w"], w["conv_b"], w["conv_b"], w["w_down"], final_norm)


def _rope_tables(seq):
    def cs(dim):
        inv = ROPE_THETA ** (-jnp.arange(0, dim, 2, dtype=F32) / dim)
        ang = jnp.arange(seq, dtype=F32)[:, None] * inv[None, :]
        return jnp.cos(ang), jnp.sin(ang)

    def table(cos, sin, offsets):
        half = cos.shape[1]
        c = jnp.ones((seq, LANES), F32)
        sa = jnp.zeros((seq, LANES), F32)
        sb = jnp.zeros((seq, LANES), F32)
        for o in offsets:
            c = c.at[:, o:o + half].set(cos).at[:, o + half:o + 2 * half].set(cos)
            sa = sa.at[:, o:o + half].set(-sin)
            sb = sb.at[:, o + half:o + 2 * half].set(sin)
        return jnp.stack([c, sa, sb])

    cm, sm = cs(MLA_ROPE_DIM)
    cn, sn = cs(NSA_ROT_DIM)
    return {"mla": table(cm, sm, (MLA_NOPE_DIM,)),
            "nq": table(cn, sn, (0, NSA_HEAD_DIM)),
            "nk": table(cn, sn, (0,))}


def _select_tables(seq):
    nc = seq // CMP_STRIDE
    n_c = (seq - CMP_LEN) // CMP_STRIDE + 1
    n_s = seq // SLC_LEN
    c = np.arange(nc)[:, None]
    sb = np.arange(LANES)[None, :]
    cs, ce = c * CMP_STRIDE, c * CMP_STRIDE + CMP_LEN - 1
    ss = sb * SLC_LEN
    ov = (cs <= ss + SLC_LEN - 1) & (ce >= ss) & (c < n_c) & (sb < n_s)
    ex = (np.arange(seq)[None, :] // SLC_LEN) == np.arange(LANES)[:, None]
    return jnp.asarray(ov, BF16), jnp.asarray(ex, BF16)


def _layer_weights(l, attn_norm, w_in, mla_q_norm, mla_kv_norm, mla_w_uq, mla_w_ukv, nsa_gate_bias,
                   nsa_cmp_pos, nsa_cmp_w1, nsa_cmp_b1, nsa_cmp_w2, w_out, ffn_norm, ffn_w_up,
                   ffn_conv_w, ffn_conv_b, ffn_w_down):
    d = D_MODEL
    wi = w_in[l]
    z = lambda n: jnp.zeros((d, n), F32)
    kpe = jnp.concatenate([z(MLA_NOPE_DIM), wi[:, _OFF[2]:_OFF[3]], z(LANES - MLA_QK_DIM)], axis=1)
    nkv = wi[:, _OFF[4]:_OFF[5]].reshape(d, NSA_BRANCHES, 2, NSA_KV_HEADS, NSA_HEAD_DIM)
    nkv = nkv.transpose(0, 1, 3, 2, 4).reshape(d, -1)
    gpg = NSA_GROUP * NSA_BRANCHES
    gw = wi[:, _OFF[5]:_OFF[6]].reshape(d, NSA_KV_HEADS, gpg)
    gw = jnp.pad(gw, ((0, 0), (0, 0), (0, LANES - gpg))).reshape(d, -1)
    w_in_r = jnp.concatenate([wi[:, _OFF[0]:_OFF[2]], kpe, wi[:, _OFF[3]:_OFF[4]], nkv, gw], axis=1)
    gb = jnp.pad(nsa_gate_bias[l].reshape(NSA_KV_HEADS, gpg), ((0, 0), (0, LANES - gpg))).reshape(1, -1)

    wuq = mla_w_uq[l].reshape(MLA_Q_RANK, MLA_HEADS, MLA_QK_DIM)
    wuq = jnp.pad(wuq, ((0, 0), (0, 0), (0, LANES - MLA_QK_DIM))).reshape(MLA_Q_RANK, -1)
    wukv = mla_w_ukv[l].reshape(MLA_KV_RANK, MLA_HEADS, 2, MLA_NOPE_DIM)
    wukv = jnp.pad(wukv, ((0, 0), (0, 0), (0, 0), (0, LANES - MLA_NOPE_DIM))).reshape(MLA_KV_RANK, -1)

    pos = jnp.concatenate([nsa_cmp_pos[l, 0], nsa_cmp_pos[l, 1]], axis=1)
    w1 = nsa_cmp_w1[l].reshape(2, CMP_LEN, NSA_HEAD_DIM, CMP_HIDDEN)
    zz = jnp.zeros_like(w1[0])
    w1c = jnp.concatenate([jnp.concatenate([w1[0], zz], axis=2), jnp.concatenate([zz, w1[1]], axis=2)], axis=1)
    b1c = nsa_cmp_b1[l].reshape(1, -1)
    w2 = nsa_cmp_w2[l]
    z2 = jnp.zeros_like(w2[0])
    w2c = jnp.concatenate([jnp.concatenate([w2[0], z2], axis=1), jnp.concatenate([z2, w2[1]], axis=1)], axis=0)

    wo = w_out[l]
    nm = MLA_HEADS * MLA_V_DIM
    wo_m = jnp.pad(wo[:nm].reshape(MLA_HEADS, MLA_V_DIM, d), ((0, 0), (0, LANES - MLA_V_DIM), (0, 0)))
    wo_n = jnp.pad(wo[nm:].reshape(NSA_HEADS, NSA_HEAD_DIM, d), ((0, 0), (LANES - NSA_HEAD_DIM, 0), (0, 0)))
    return {
        "attn_norm": attn_norm[l].reshape(1, d), "w_in": w_in_r.astype(BF16),
        "q_norm": mla_q_norm[l].reshape(1, -1), "kv_norm": mla_kv_norm[l].reshape(1, -1),
        "w_uq": wuq.astype(BF16), "w_ukv": wukv.astype(BF16), "gate_bias": gb,
        "cmp_pos": pos, "cmp_w1": w1c.astype(BF16), "cmp_b1": b1c, "cmp_w2": w2c.astype(BF16),
        "wo_m": wo_m.reshape(-1, d).astype(BF16), "wo_n": wo_n.reshape(-1, d).astype(BF16),
        "ffn_norm": ffn_norm[l].reshape(1, d), "w_up": ffn_w_up[l].astype(BF16),
        "conv_w": ffn_conv_w[l], "conv_b": ffn_conv_b[l].reshape(1, -1), "w_down": ffn_w_down[l].astype(BF16),
    }


def kernel(x, attn_norm, w_in, mla_q_norm, mla_kv_norm, mla_w_uq, mla_w_ukv, nsa_gate_bias, nsa_cmp_pos, nsa_cmp_w1, nsa_cmp_b1, nsa_cmp_w2, w_out, ffn_norm, ffn_w_up, ffn_conv_w, ffn_conv_b, ffn_w_down, final_norm):
    b, s, d = x.shape
    depth = w_in.shape[0]
    assert d == D_MODEL and s % TQ_MLA == 0 and s >= WINDOW + TQ_NSA and s // SLC_LEN <= LANES
    tabs = _rope_tables(s)
    ov, ex = _select_tables(s)
    fn = final_norm.reshape(1, d)
    for l in range(depth):
        w = _layer_weights(l, attn_norm, w_in, mla_q_norm, mla_kv_norm, mla_w_uq, mla_w_ukv, nsa_gate_bias,
                           nsa_cmp_pos, nsa_cmp_w1, nsa_cmp_b1, nsa_cmp_w2, w_out, ffn_norm, ffn_w_up,
                           ffn_conv_w, ffn_conv_b, ffn_w_down)
        mq, mkv, nq, nkv, ncmp, gates = _proj_call(x, w, tabs)
        kvc = _cmp_call(ncmp, w)
        o_mla = _mla_call(mq, mkv)
        o_nsa = _nsa_call(nq, kvc, nkv, gates, ov, ex)
        x = _ffn_call(x, o_mla, o_nsa, w, fn, final=(l == depth - 1))
    return x
```

```python
import functools

import jax
import jax.numpy as jnp
import numpy as np
from jax import lax
from jax.experimental import pallas as pl
from jax.experimental.pallas import tpu as pltpu

F32 = jnp.float32
BF16 = jnp.bfloat16

LANES = 128

D_MODEL = 1024
ROPE_THETA = 500000.0
NORM_EPS = 1e-6
NEG_INF = -1e30
LOG2E = 1.4426950408889634

MLA_HEADS = 8
MLA_Q_RANK = 256
MLA_KV_RANK = 128
MLA_NOPE_DIM = 64
MLA_ROPE_DIM = 32
MLA_V_DIM = 64
MLA_QK_DIM = MLA_NOPE_DIM + MLA_ROPE_DIM

NSA_HEADS = 8
NSA_KV_HEADS = 2
NSA_GROUP = NSA_HEADS // NSA_KV_HEADS
NSA_HEAD_DIM = 64
NSA_ROT_DIM = NSA_HEAD_DIM // 4
NSA_BRANCHES = 3
CMP_LEN = 32
CMP_STRIDE = 16
CMP_HIDDEN = 128
SLC_LEN = 64
SLC_TOPK = 16
WINDOW = 512
FORCE_SCORE = 1e6

D_FF = 2816
CONV_WIDTH = 3

IN_SPLITS = (MLA_Q_RANK, MLA_KV_RANK, MLA_ROPE_DIM, NSA_HEADS * NSA_HEAD_DIM,
             NSA_BRANCHES * 2 * NSA_KV_HEADS * NSA_HEAD_DIM, NSA_HEADS * NSA_BRANCHES)
_OFF = tuple(int(v) for v in np.cumsum((0,) + IN_SPLITS))

C_QLAT = 0
C_KVLAT = C_QLAT + MLA_Q_RANK
C_KPE = C_KVLAT + MLA_KV_RANK
C_NQ = C_KPE + LANES
C_NKV = C_NQ + NSA_HEADS * NSA_HEAD_DIM
C_GATE = C_NKV + NSA_BRANCHES * NSA_KV_HEADS * LANES
C_END = C_GATE + NSA_KV_HEADS * LANES

VMEM_LIMIT = 56 * 1024 * 1024

TM_PROJ = 512
TQ_MLA = 512
TQ_NSA = 256
TK_SLC = 512
TM_FFN = 512
TF_FFN = 1408
HALO = 16


def _rms(x, g):
    return x * lax.rsqrt(jnp.mean(x * x, axis=-1, keepdims=True) + NORM_EPS) * g


def _rope(x, tab_ref, half):
    return (x * tab_ref[0] + pltpu.roll(x, LANES - half, 1) * tab_ref[1]
            + pltpu.roll(x, half, 1) * tab_ref[2])


def _dot(a, b):
    return jnp.dot(a, b, preferred_element_type=F32)


def _dot_nt(a, b):
    return lax.dot_general(a, b, (((1,), (1,)), ((), ())), preferred_element_type=F32)


def _proj_kernel(x_ref, g_ref, win_ref, qn_ref, kvn_ref, wuq_ref, wukv_ref, gb_ref,
                 tmla_ref, tnq_ref, tnk_ref,
                 mq_ref, mkv_ref, nq_ref, nk_ref, nv_ref, ncmp_ref, gate_ref):
    h = _rms(x_ref[0], g_ref[...])
    p = _dot(h.astype(BF16), win_ref[...])

    ql = _rms(p[:, C_QLAT:C_QLAT + MLA_Q_RANK], qn_ref[...])
    q = _dot(ql.astype(BF16), wuq_ref[...])
    scale = MLA_QK_DIM ** -0.5 * LOG2E
    for hh in range(MLA_HEADS):
        blk = q[:, hh * LANES:(hh + 1) * LANES]
        mq_ref[0, :, hh * LANES:(hh + 1) * LANES] = (
            _rope(blk, tmla_ref, MLA_ROPE_DIM // 2) * scale).astype(BF16)

    kpe = _rope(p[:, C_KPE:C_KPE + LANES], tmla_ref, MLA_ROPE_DIM // 2)
    kvl = _rms(p[:, C_KVLAT:C_KVLAT + MLA_KV_RANK], kvn_ref[...])
    kv = _dot(kvl.astype(BF16), wukv_ref[...])
    ones_hi = jnp.where(lax.broadcasted_iota(jnp.int32, kpe.shape, 1) >= MLA_V_DIM, 1.0, 0.0)
    for hh in range(MLA_HEADS):
        c0 = hh * 2 * LANES
        mkv_ref[0, :, c0:c0 + LANES] = (kv[:, c0:c0 + LANES] + kpe).astype(BF16)
        mkv_ref[0, :, c0 + LANES:c0 + 2 * LANES] = (kv[:, c0 + LANES:c0 + 2 * LANES] + ones_hi).astype(BF16)

    nscale = NSA_HEAD_DIM ** -0.5 * LOG2E
    for j in range(NSA_HEADS * NSA_HEAD_DIM // LANES):
        blk = p[:, C_NQ + j * LANES:C_NQ + (j + 1) * LANES]
        nq_ref[0, :, j * LANES:(j + 1) * LANES] = (
            _rope(blk, tnq_ref, NSA_ROT_DIM // 2) * nscale).astype(BF16)

    low = lax.broadcasted_iota(jnp.int32, kpe.shape, 1) < NSA_HEAD_DIM
    for j in range(NSA_BRANCHES * NSA_KV_HEADS):
        blk = _rope(p[:, C_NKV + j * LANES:C_NKV + (j + 1) * LANES], tnk_ref, NSA_ROT_DIM // 2)
        if j < NSA_KV_HEADS:
            ncmp_ref[0, :, j * LANES:(j + 1) * LANES] = blk
        else:
            jj = j - NSA_KV_HEADS
            nk_ref[0, :, jj * LANES:(jj + 1) * LANES] = jnp.where(low, blk, 0.0).astype(BF16)
            nv_ref[0, :, jj * LANES:(jj + 1) * LANES] = jnp.where(
                low, pltpu.roll(blk, NSA_HEAD_DIM, 1), 1.0).astype(BF16)

    gate_ref[0] = jax.nn.sigmoid(p[:, C_GATE:C_END] + gb_ref[...])


def _proj_call(x, w, tabs):
    b, s, d = x.shape
    tm = TM_PROJ
    const = lambda shape: pl.BlockSpec(shape, lambda i, bb: (0,) * len(shape))
    tab = pl.BlockSpec((3, tm, LANES), lambda i, bb: (0, i, 0))
    row = lambda width: pl.BlockSpec((1, tm, width), lambda i, bb: (bb, i, 0))
    return pl.pallas_call(
        _proj_kernel,
        grid=(s // tm, b),
        in_specs=[row(d), const((1, d)), const((d, C_END)), const((1, MLA_Q_RANK)),
                  const((1, MLA_KV_RANK)), const((MLA_Q_RANK, MLA_HEADS * LANES)),
                  const((MLA_KV_RANK, MLA_HEADS * 2 * LANES)), const((1, NSA_KV_HEADS * LANES)),
                  tab, tab, tab],
        out_specs=[row(MLA_HEADS * LANES), row(MLA_HEADS * 2 * LANES), row(NSA_HEADS * NSA_HEAD_DIM),
                   row(2 * NSA_KV_HEADS * LANES), row(2 * NSA_KV_HEADS * LANES),
                   row(NSA_KV_HEADS * LANES), row(NSA_KV_HEADS * LANES)],
        out_shape=[jax.ShapeDtypeStruct((b, s, MLA_HEADS * LANES), BF16),
                   jax.ShapeDtypeStruct((b, s, MLA_HEADS * 2 * LANES), BF16),
                   jax.ShapeDtypeStruct((b, s, NSA_HEADS * NSA_HEAD_DIM), BF16),
                   jax.ShapeDtypeStruct((b, s, 2 * NSA_KV_HEADS * LANES), BF16),
                   jax.ShapeDtypeStruct((b, s, 2 * NSA_KV_HEADS * LANES), BF16),
                   jax.ShapeDtypeStruct((b, s, NSA_KV_HEADS * LANES), F32),
                   jax.ShapeDtypeStruct((b, s, NSA_KV_HEADS * LANES), F32)],
        compiler_params=pltpu.CompilerParams(
            dimension_semantics=("arbitrary", "arbitrary"), vmem_limit_bytes=VMEM_LIMIT),
        name="proj",
    )(x, w["attn_norm"], w["w_in"], w["q_norm"], w["kv_norm"], w["w_uq"], w["w_ukv"], w["gate_bias"],
      tabs["mla"], tabs["nq"], tabs["nk"])


def _cmp_kernel(x_ref, pos_ref, w1_ref, b1_ref, w2_ref, kc_ref, vc_ref):
    nc = kc_ref.shape[2]
    half = CMP_LEN // 2
    a = jnp.zeros((nc, 2 * CMP_HIDDEN), F32)
    bm = jnp.zeros((nc, 2 * CMP_HIDDEN), F32)
    for l in range(half):
        xl = x_ref[0, pl.ds(l, nc, stride=CMP_STRIDE), :]
        a = a + _dot((xl + pos_ref[l:l + 1, :]).astype(BF16), w1_ref[l])
        bm = bm + _dot((xl + pos_ref[half + l:half + l + 1, :]).astype(BF16), w1_ref[half + l])
    hid = a + pltpu.roll(bm, nc - 1, 0) + b1_ref[...]
    r = _dot(jax.nn.gelu(hid).astype(BF16), w2_ref[...])
    low = lax.broadcasted_iota(jnp.int32, r.shape, 1) < NSA_HEAD_DIM
    kc_ref[0, 0] = jnp.where(low, r, 0.0).astype(BF16)
    vc_ref[0, 0] = jnp.where(low, pltpu.roll(r, NSA_HEAD_DIM, 1), 1.0).astype(BF16)


def _cmp_call(ncmp, w):
    b, s, _ = ncmp.shape
    nc = s // CMP_STRIDE
    const = lambda shape: pl.BlockSpec(shape, lambda bb, g: (0,) * len(shape))
    return pl.pallas_call(
        _cmp_kernel,
        grid=(b, NSA_KV_HEADS),
        in_specs=[pl.BlockSpec((1, s, LANES), lambda bb, g: (bb, 0, g)),
                  const((CMP_LEN, LANES)), const((CMP_LEN, LANES, 2 * CMP_HIDDEN)),
                  const((1, 2 * CMP_HIDDEN)), const((2 * CMP_HIDDEN, LANES))],
        out_specs=[pl.BlockSpec((1, 1, nc, LANES), lambda bb, g: (bb, g, 0, 0))] * 2,
        out_shape=[jax.ShapeDtypeStruct((b, NSA_KV_HEADS, nc, LANES), BF16)] * 2,
        compiler_params=pltpu.CompilerParams(
            dimension_semantics=("arbitrary", "arbitrary"), vmem_limit_bytes=VMEM_LIMIT),
        name="compress",
    )(ncmp, w["cmp_pos"], w["cmp_w1"], w["cmp_b1"], w["cmp_w2"])


def _mla_kernel(q_ref, kv_ref, o_ref, m_ref, acc_ref):
    tq = q_ref.shape[1]
    tk = tq
    qi = pl.program_id(2)
    q = q_ref[0]
    m_ref[...] = jnp.full(m_ref.shape, NEG_INF, F32)
    acc_ref[...] = jnp.zeros(acc_ref.shape, F32)

    def step(j, masked):
        k0 = pl.multiple_of(j * tk, tk)
        k = kv_ref[0, pl.ds(k0, tk), 0:LANES]
        v = kv_ref[0, pl.ds(k0, tk), LANES:2 * LANES]
        s = _dot_nt(q, k)
        if masked:
            s = jnp.where(lax.broadcasted_iota(jnp.int32, s.shape, 1)
                          <= lax.broadcasted_iota(jnp.int32, s.shape, 0), s, NEG_INF)
        m_old = m_ref[...]
        m_new = jnp.maximum(m_old, jnp.max(s, axis=1, keepdims=True))
        p = jnp.exp2(s - jnp.tile(m_new, (1, tk // LANES)))
        acc_ref[...] = jnp.exp2(m_old - m_new) * acc_ref[...] + _dot(p.astype(BF16), v)
        m_ref[...] = m_new

    def body(j, c):
        step(j, False)
        return c

    lax.fori_loop(0, qi, body, 0)
    step(qi, True)
    acc = acc_ref[...]
    lane = lax.broadcasted_iota(jnp.int32, acc.shape, 1)
    denom = jnp.where(lane < MLA_V_DIM, pltpu.roll(acc, MLA_V_DIM, 1), acc)
    o_ref[0] = (acc / denom).astype(BF16)


def _mla_call(mq, mkv):
    b, s, _ = mq.shape
    tq = TQ_MLA
    return pl.pallas_call(
        _mla_kernel,
        grid=(b, MLA_HEADS, s // tq),
        in_specs=[pl.BlockSpec((1, tq, LANES), lambda bb, h, i: (bb, i, h)),
                  pl.BlockSpec((1, s, 2 * LANES), lambda bb, h, i: (bb, 0, h))],
        out_specs=pl.BlockSpec((1, tq, LANES), lambda bb, h, i: (bb, i, h)),
        out_shape=jax.ShapeDtypeStruct((b, s, MLA_HEADS * LANES), BF16),
        scratch_shapes=[pltpu.VMEM((tq, LANES), F32), pltpu.VMEM((tq, LANES), F32)],
        compiler_params=pltpu.CompilerParams(
            dimension_semantics=("arbitrary", "arbitrary", "arbitrary"), vmem_limit_bytes=VMEM_LIMIT),
        name="mla_attn",
    )(mq, mkv)


def _nsa_kernel(q_ref, kc_ref, vc_ref, ks_ref, vs_ref, kw_ref, vw_ref, gate_ref, ovt_ref, ext_ref, o_ref,
                m_ref, acc_ref, *, seq):
    tq = q_ref.shape[1]
    hg = NSA_GROUP
    nc = kc_ref.shape[2]
    tk = TK_SLC
    q0 = pl.program_id(2) * tq

    xq = q_ref[0].astype(F32)
    parts = []
    for j in range(hg * NSA_HEAD_DIM // LANES):
        a = xq[:, j * LANES:(j + 1) * LANES]
        parts.append(a)
        parts.append(pltpu.roll(a, NSA_HEAD_DIM, 1))
    qst = jnp.concatenate(parts, axis=0).astype(BF16)

    def normalized(acc):
        lane = lax.broadcasted_iota(jnp.int32, acc.shape, 1)
        return acc / jnp.where(lane < NSA_HEAD_DIM, pltpu.roll(acc, NSA_HEAD_DIM, 1), acc)

    kc = kc_ref[0, 0]
    vc = vc_ref[0, 0]
    sc = _dot_nt(qst, kc)
    qpos_c = q0 + lax.broadcasted_iota(jnp.int32, (tq, nc), 0)
    cend = lax.broadcasted_iota(jnp.int32, (tq, nc), 1) * CMP_STRIDE + (CMP_LEN - 1)
    mask_c = cend <= qpos_c
    pcs = jnp.zeros((tq, nc), F32)
    o_cmp = []
    for h in range(hg):
        sh = jnp.where(mask_c, sc[h * tq:(h + 1) * tq], NEG_INF)
        e = jnp.exp2(sh - jnp.max(sh, axis=1, keepdims=True))
        pc = jnp.where(mask_c, e, 0.0) * (1.0 / jnp.sum(e, axis=1, keepdims=True))
        pcs = pcs + pc
        o_cmp.append(_dot(pc.astype(BF16), vc))
    ovt = ovt_ref[...]
    p_hi = pcs.astype(BF16)
    r1 = pcs - p_hi.astype(F32)
    p_mid = r1.astype(BF16)
    p_lo = (r1 - p_mid.astype(F32)).astype(BF16)
    imp = _dot_nt(ovt, p_hi) + _dot_nt(ovt, p_mid) + _dot_nt(ovt, p_lo)

    blk = lax.broadcasted_iota(jnp.int32, (LANES, tq), 0)
    cur = (q0 + lax.broadcasted_iota(jnp.int32, (LANES, tq), 1)) // SLC_LEN
    forced = (blk == 0) | (blk == cur) | (blk == cur - 1)
    allowed = blk <= cur
    wl = WINDOW + tq
    start = pl.multiple_of(jnp.maximum(q0 - WINDOW, 0), tq)
    sw = _dot_nt(qst, kw_ref[0, pl.ds(start, wl), :])
    vw = vw_ref[0, pl.ds(start, wl), :]
    qpos_w = q0 + lax.broadcasted_iota(jnp.int32, (tq, wl), 0)
    kpos_w = start + lax.broadcasted_iota(jnp.int32, (tq, wl), 1)
    mask_w = jnp.where(kpos_w <= qpos_w, kpos_w, -WINDOW - seq) > qpos_w - WINDOW
    o_win = []
    for h in range(hg):
        sh = jnp.where(mask_w, sw[h * tq:(h + 1) * tq], NEG_INF)
        e = jnp.exp2(sh - jnp.max(sh, axis=1, keepdims=True))
        o_win.append(normalized(_dot(e.astype(BF16), vw)))

    score = jnp.where(forced, -2.0, jnp.where(allowed, imp, -1.0))
    blk_f = blk.astype(F32)
    bias = jnp.where(forced, 0.0, NEG_INF)
    for _ in range(SLC_TOPK - 3):
        mx = jnp.max(score, axis=0, keepdims=True)
        first = jnp.min(jnp.where(score == mx, blk_f, float(LANES)), axis=0, keepdims=True)
        hit = blk_f == first
        bias = jnp.where(hit, 0.0, bias)
        score = jnp.where(hit, -2.0, score)
    selneg = jnp.where(allowed, bias, NEG_INF).T.astype(BF16)
    qaug = jnp.concatenate([qst, jnp.concatenate([selneg] * hg, axis=0)], axis=1)

    m_ref[...] = jnp.full(m_ref.shape, NEG_INF, F32)
    acc_ref[...] = jnp.zeros(acc_ref.shape, F32)

    def slc_step(j, masked):
        k0 = pl.multiple_of(j * tk, tk)
        kaug = jnp.concatenate([ks_ref[0, pl.ds(k0, tk), :], ext_ref[pl.ds(k0, tk), :]], axis=1)
        s = _dot_nt(qaug, kaug)
        v = vs_ref[0, pl.ds(k0, tk), :]
        if masked:
            causal = (k0 + lax.broadcasted_iota(jnp.int32, (tq, tk), 1)
                      <= q0 + lax.broadcasted_iota(jnp.int32, (tq, tk), 0))
        for h in range(hg):
            rows = slice(h * tq, (h + 1) * tq)
            sh = s[rows]
            if masked:
                sh = jnp.where(causal, sh, NEG_INF)
            m_old = m_ref[rows]
            m_new = jnp.maximum(m_old, jnp.max(sh, axis=1, keepdims=True))
            p = jnp.exp2(sh - jnp.tile(m_new, (1, tk // LANES)))
            acc_ref[rows] = jnp.exp2(m_old - m_new) * acc_ref[rows] + _dot(p.astype(BF16), v)
            m_ref[rows] = m_new

    def slc_body(j, c):
        slc_step(j, False)
        return c

    n_full = q0 // tk
    lax.fori_loop(0, n_full, slc_body, 0)
    slc_step(n_full, True)

    gates = gate_ref[0]
    for h in range(hg):
        o_slc = normalized(acc_ref[h * tq:(h + 1) * tq])
        c = h * NSA_BRANCHES
        o = gates[:, c:c + 1] * o_cmp[h] + gates[:, c + 1:c + 2] * o_slc + gates[:, c + 2:c + 3] * o_win[h]
        o_ref[0, :, h * LANES:(h + 1) * LANES] = o.astype(BF16)


def _nsa_call(nq, kc, vc, nk, nv, gates, ov, ext):
    b, s, _ = nq.shape
    tq = TQ_NSA
    hg = NSA_GROUP
    nc = kc.shape[2]
    cmp_spec = pl.BlockSpec((1, 1, nc, LANES), lambda bb, g, i: (bb, g, 0, 0))
    slc_spec = pl.BlockSpec((1, s, LANES), lambda bb, g, i: (bb, 0, g))
    win_spec = pl.BlockSpec((1, s, LANES), lambda bb, g, i: (bb, 0, NSA_KV_HEADS + g))
    return pl.pallas_call(
        functools.partial(_nsa_kernel, seq=s),
        grid=(b, NSA_KV_HEADS, s // tq),
        in_specs=[pl.BlockSpec((1, tq, hg * NSA_HEAD_DIM), lambda bb, g, i: (bb, i, g)),
                  cmp_spec, cmp_spec, slc_spec, slc_spec, win_spec, win_spec,
                  pl.BlockSpec((1, tq, LANES), lambda bb, g, i: (bb, i, g)),
                  pl.BlockSpec((LANES, nc), lambda bb, g, i: (0, 0)),
                  pl.BlockSpec((s, LANES), lambda bb, g, i: (0, 0))],
        out_specs=pl.BlockSpec((1, tq, hg * LANES), lambda bb, g, i: (bb, i, g)),
        out_shape=jax.ShapeDtypeStruct((b, s, NSA_HEADS * LANES), BF16),
        scratch_shapes=[pltpu.VMEM((hg * tq, LANES), F32), pltpu.VMEM((hg * tq, LANES), F32)],
        compiler_params=pltpu.CompilerParams(
            dimension_semantics=("arbitrary", "arbitrary", "arbitrary"), vmem_limit_bytes=VMEM_LIMIT),
        name="nsa_attn",
    )(nq, kc, vc, nk, nv, nk, nv, gates, ov, ext)


def _ffn_kernel(x_ref, xh_ref, am_ref, amh_ref, an_ref, anh_ref, wom_ref, won_ref, g_ref,
                wu_ref, wv_ref, cwu_ref, cwv_ref, cbu_ref, cbv_ref, wd_ref, fn_ref, o_ref,
                x1_ref, hn_ref, acc_ref, *, final):
    tm = x_ref.shape[1]
    i = pl.program_id(1)
    f = pl.program_id(2)

    @pl.when(f == 0)
    def _():
        def piece(xv, am, an):
            x1 = xv + _dot(am, wom_ref[...]) + _dot(an, won_ref[...])
            return x1, _rms(x1, g_ref[...]).astype(BF16)

        x1, hn = piece(x_ref[0], am_ref[0], an_ref[0])
        x1_ref[...] = x1
        hn_ref[HALO:, :] = hn
        _, hnh = piece(xh_ref[0], amh_ref[0], anh_ref[0])
        hn_ref[0:HALO, :] = jnp.where(i > 0, hnh, jnp.zeros_like(hnh))
        acc_ref[...] = jnp.zeros(acc_ref.shape, F32)

    hn = hn_ref[...]

    def conv(w_ref, cw_ref, cb_ref):
        hh = _dot(hn, w_ref[...])
        return (hh[HALO:] * cw_ref[2:3, :] + hh[HALO - 1:HALO - 1 + tm] * cw_ref[1:2, :]
                + hh[HALO - 2:HALO - 2 + tm] * cw_ref[0:1, :] + cb_ref[...])

    u = conv(wu_ref, cwu_ref, cbu_ref)
    v = conv(wv_ref, cwv_ref, cbv_ref)
    acc_ref[...] += _dot((u * jax.nn.sigmoid(u) * v).astype(BF16), wd_ref[...])

    @pl.when(f == pl.num_programs(2) - 1)
    def _():
        y = x1_ref[...] + acc_ref[...]
        if final:
            y = _rms(y, fn_ref[...])
        o_ref[0] = y


def _ffn_call(x, am, an, w, final_norm, final):
    b, s, d = x.shape
    tm, tf = TM_FFN, TF_FFN
    nf = D_FF // tf
    hb = tm // HALO
    const = lambda shape: pl.BlockSpec(shape, lambda bb, i, f: (0,) * len(shape))
    row = lambda width: pl.BlockSpec((1, tm, width), lambda bb, i, f: (bb, i, 0))
    halo = lambda width: pl.BlockSpec((1, HALO, width), lambda bb, i, f: (bb, jnp.maximum(i * hb - 1, 0), 0))
    na = am.shape[2]
    return pl.pallas_call(
        functools.partial(_ffn_kernel, final=final),
        grid=(b, s // tm, nf),
        in_specs=[row(d), halo(d), row(na), halo(na), row(na), halo(na),
                  const((na, d)), const((na, d)), const((1, d)),
                  pl.BlockSpec((d, tf), lambda bb, i, f: (0, f)),
                  pl.BlockSpec((d, tf), lambda bb, i, f: (0, nf + f)),
                  pl.BlockSpec((CONV_WIDTH, tf), lambda bb, i, f: (0, f)),
                  pl.BlockSpec((CONV_WIDTH, tf), lambda bb, i, f: (0, nf + f)),
                  pl.BlockSpec((1, tf), lambda bb, i, f: (0, f)),
                  pl.BlockSpec((1, tf), lambda bb, i, f: (0, nf + f)),
                  pl.BlockSpec((tf, d), lambda bb, i, f: (f, 0)),
                  const((1, d))],
        out_specs=row(d),
        out_shape=jax.ShapeDtypeStruct((b, s, d), F32),
        scratch_shapes=[pltpu.VMEM((tm, d), F32), pltpu.VMEM((HALO + tm, d), BF16), pltpu.VMEM((tm, d), F32)],
        compiler_params=pltpu.CompilerParams(
            dimension_semantics=("arbitrary", "arbitrary", "arbitrary"), vmem_limit_bytes=VMEM_LIMIT),
        name="out_ffn",
    )(x, x, am, am, an, an, w["wo_m"], w["wo_n"], w["ffn_norm"], w["w_up"], w["w_up"],
      w["conv_w"], w["conv_w"], w["conv_b"], w["conv_b"], w["w_down"], final_norm)


def _rope_tables(seq):
    def cs(dim):
        inv = ROPE_THETA ** (-jnp.arange(0, dim, 2, dtype=F32) / dim)
        ang = jnp.arange(seq, dtype=F32)[:, None] * inv[None, :]
        return jnp.cos(ang), jnp.sin(ang)

    def table(cos, sin, offsets):
        half = cos.shape[1]
        c = jnp.ones((seq, LANES), F32)
        sa = jnp.zeros((seq, LANES), F32)
        sb = jnp.zeros((seq, LANES), F32)
        for o in offsets:
            c = c.at[:, o:o + half].set(cos).at[:, o + half:o + 2 * half].set(cos)
            sa = sa.at[:, o:o + half].set(-sin)
            sb = sb.at[:, o + half:o + 2 * half].set(sin)
        return jnp.stack([c, sa, sb])

    cm, sm = cs(MLA_ROPE_DIM)
    cn, sn = cs(NSA_ROT_DIM)
    return {"mla": table(cm, sm, (MLA_NOPE_DIM,)),
            "nq": table(cn, sn, (0, NSA_HEAD_DIM)),
            "nk": table(cn, sn, (0,))}


def _select_tables(seq):
    nc = seq // CMP_STRIDE
    n_c = (seq - CMP_LEN) // CMP_STRIDE + 1
    n_s = seq // SLC_LEN
    c = np.arange(nc)[:, None]
    sb = np.arange(LANES)[None, :]
    cs, ce = c * CMP_STRIDE, c * CMP_STRIDE + CMP_LEN - 1
    ss = sb * SLC_LEN
    ov = (cs <= ss + SLC_LEN - 1) & (ce >= ss) & (c < n_c) & (sb < n_s)
    ext = (np.arange(seq)[:, None] // SLC_LEN) == np.arange(LANES)[None, :]
    return jnp.asarray(ov.T, BF16), jnp.asarray(ext, BF16)


def _layer_weights(l, attn_norm, w_in, mla_q_norm, mla_kv_norm, mla_w_uq, mla_w_ukv, nsa_gate_bias,
                   nsa_cmp_pos, nsa_cmp_w1, nsa_cmp_b1, nsa_cmp_w2, w_out, ffn_norm, ffn_w_up,
                   ffn_conv_w, ffn_conv_b, ffn_w_down):
    d = D_MODEL
    wi = w_in[l]
    z = lambda n: jnp.zeros((d, n), F32)
    kpe = jnp.concatenate([z(MLA_NOPE_DIM), wi[:, _OFF[2]:_OFF[3]], z(LANES - MLA_QK_DIM)], axis=1)
    nkv = wi[:, _OFF[4]:_OFF[5]].reshape(d, NSA_BRANCHES, 2, NSA_KV_HEADS, NSA_HEAD_DIM)
    nkv = nkv.transpose(0, 1, 3, 2, 4).reshape(d, -1)
    gpg = NSA_GROUP * NSA_BRANCHES
    gw = wi[:, _OFF[5]:_OFF[6]].reshape(d, NSA_KV_HEADS, gpg)
    gw = jnp.pad(gw, ((0, 0), (0, 0), (0, LANES - gpg))).reshape(d, -1)
    w_in_r = jnp.concatenate([wi[:, _OFF[0]:_OFF[2]], kpe, wi[:, _OFF[3]:_OFF[4]], nkv, gw], axis=1)
    gb = jnp.pad(nsa_gate_bias[l].reshape(NSA_KV_HEADS, gpg), ((0, 0), (0, LANES - gpg))).reshape(1, -1)

    wuq = mla_w_uq[l].reshape(MLA_Q_RANK, MLA_HEADS, MLA_QK_DIM)
    wuq = jnp.pad(wuq, ((0, 0), (0, 0), (0, LANES - MLA_QK_DIM))).reshape(MLA_Q_RANK, -1)
    wukv = mla_w_ukv[l].reshape(MLA_KV_RANK, MLA_HEADS, 2, MLA_NOPE_DIM)
    wukv = jnp.pad(wukv, ((0, 0), (0, 0), (0, 0), (0, LANES - MLA_NOPE_DIM))).reshape(MLA_KV_RANK, -1)

    pos = jnp.concatenate([nsa_cmp_pos[l, 0], nsa_cmp_pos[l, 1]], axis=1)
    w1 = nsa_cmp_w1[l].reshape(2, CMP_LEN, NSA_HEAD_DIM, CMP_HIDDEN)
    zz = jnp.zeros_like(w1[0])
    w1c = jnp.concatenate([jnp.concatenate([w1[0], zz], axis=2), jnp.concatenate([zz, w1[1]], axis=2)], axis=1)
    b1c = nsa_cmp_b1[l].reshape(1, -1)
    w2 = nsa_cmp_w2[l]
    z2 = jnp.zeros_like(w2[0])
    w2c = jnp.concatenate([jnp.concatenate([w2[0], z2], axis=1), jnp.concatenate([z2, w2[1]], axis=1)], axis=0)

    wo = w_out[l]
    nm = MLA_HEADS * MLA_V_DIM
    wo_m = jnp.pad(wo[:nm].reshape(MLA_HEADS, MLA_V_DIM, d), ((0, 0), (0, LANES - MLA_V_DIM), (0, 0)))
    wo_n = jnp.pad(wo[nm:].reshape(NSA_HEADS, NSA_HEAD_DIM, d), ((0, 0), (0, LANES - NSA_HEAD_DIM), (0, 0)))
    return {
        "attn_norm": attn_norm[l].reshape(1, d), "w_in": w_in_r.astype(BF16),
        "q_norm": mla_q_norm[l].reshape(1, -1), "kv_norm": mla_kv_norm[l].reshape(1, -1),
        "w_uq": wuq.astype(BF16), "w_ukv": wukv.astype(BF16), "gate_bias": gb,
        "cmp_pos": pos, "cmp_w1": w1c.astype(BF16), "cmp_b1": b1c, "cmp_w2": w2c.astype(BF16),
        "wo_m": wo_m.reshape(-1, d).astype(BF16), "wo_n": wo_n.reshape(-1, d).astype(BF16),
        "ffn_norm": ffn_norm[l].reshape(1, d), "w_up": ffn_w_up[l].astype(BF16),
        "conv_w": ffn_conv_w[l], "conv_b": ffn_conv_b[l].reshape(1, -1), "w_down": ffn_w_down[l].astype(BF16),
    }


def kernel(x, attn_norm, w_in, mla_q_norm, mla_kv_norm, mla_w_uq, mla_w_ukv, nsa_gate_bias, nsa_cmp_pos, nsa_cmp_w1, nsa_cmp_b1, nsa_cmp_w2, w_out, ffn_norm, ffn_w_up, ffn_conv_w, ffn_conv_b, ffn_w_down, final_norm):
    b, s, d = x.shape
    depth = w_in.shape[0]
    assert d == D_MODEL and s % TQ_MLA == 0 and s >= WINDOW + TQ_NSA and s // SLC_LEN <= LANES
    tabs = _rope_tables(s)
    ov, ext = _select_tables(s)
    fn = final_norm.reshape(1, d)
    for l in range(depth):
        w = _layer_weights(l, attn_norm, w_in, mla_q_norm, mla_kv_norm, mla_w_uq, mla_w_ukv, nsa_gate_bias,
                           nsa_cmp_pos, nsa_cmp_w1, nsa_cmp_b1, nsa_cmp_w2, w_out, ffn_norm, ffn_w_up,
                           ffn_conv_w, ffn_conv_b, ffn_w_down)
        mq, mkv, nq, nk, nv, ncmp, gates = _proj_call(x, w, tabs)
        kc, vc = _cmp_call(ncmp, w)
        o_mla = _mla_call(mq, mkv)
        o_nsa = _nsa_call(nq, kc, vc, nk, nv, gates, ov, ext)
        x = _ffn_call(x, o_mla, o_nsa, w, fn, final=(l == depth - 1))
    return x
```

```python
import functools

import jax
import jax.numpy as jnp
import numpy as np
from jax import lax
from jax.experimental import pallas as pl
from jax.experimental.pallas import tpu as pltpu

F32 = jnp.float32
BF16 = jnp.bfloat16

LANES = 128

D_MODEL = 1024
ROPE_THETA = 500000.0
NORM_EPS = 1e-6
NEG_INF = -1e30
LOG2E = 1.4426950408889634

MLA_HEADS = 8
MLA_Q_RANK = 256
MLA_KV_RANK = 128
MLA_NOPE_DIM = 64
MLA_ROPE_DIM = 32
MLA_V_DIM = 64
MLA_QK_DIM = MLA_NOPE_DIM + MLA_ROPE_DIM

NSA_HEADS = 8
NSA_KV_HEADS = 2
NSA_GROUP = NSA_HEADS // NSA_KV_HEADS
NSA_HEAD_DIM = 64
NSA_ROT_DIM = NSA_HEAD_DIM // 4
NSA_BRANCHES = 3
CMP_LEN = 32
CMP_STRIDE = 16
CMP_HIDDEN = 128
SLC_LEN = 64
SLC_TOPK = 16
WINDOW = 512
FORCE_SCORE = 1e6

D_FF = 2816
CONV_WIDTH = 3

IN_SPLITS = (MLA_Q_RANK, MLA_KV_RANK, MLA_ROPE_DIM, NSA_HEADS * NSA_HEAD_DIM,
             NSA_BRANCHES * 2 * NSA_KV_HEADS * NSA_HEAD_DIM, NSA_HEADS * NSA_BRANCHES)
_OFF = tuple(int(v) for v in np.cumsum((0,) + IN_SPLITS))

C_QLAT = 0
C_KVLAT = C_QLAT + MLA_Q_RANK
C_KPE = C_KVLAT + MLA_KV_RANK
C_NQ = C_KPE + LANES
C_NKV = C_NQ + NSA_HEADS * NSA_HEAD_DIM
C_GATE = C_NKV + NSA_BRANCHES * NSA_KV_HEADS * LANES
C_END = C_GATE + NSA_KV_HEADS * LANES

VMEM_LIMIT = 56 * 1024 * 1024

TM_PROJ = 512
TQ_MLA = 512
MLA_HEADS_PER_STEP = 2
TQ_NSA = 256
TK_SLC = 512
TM_FFN = 512
TF_FFN = 1408
HALO = 16


def _rms(x, g):
    return x * lax.rsqrt(jnp.mean(x * x, axis=-1, keepdims=True) + NORM_EPS) * g


def _rope(x, tab_ref, half):
    return (x * tab_ref[0] + pltpu.roll(x, LANES - half, 1) * tab_ref[1]
            + pltpu.roll(x, half, 1) * tab_ref[2])


def _dot(a, b):
    return jnp.dot(a, b, preferred_element_type=F32)


def _dot_nt(a, b):
    return lax.dot_general(a, b, (((1,), (1,)), ((), ())), preferred_element_type=F32)


def _proj_kernel(x_ref, g_ref, win_ref, qn_ref, kvn_ref, wuq_ref, wukv_ref, gb_ref,
                 tmla_ref, tnq_ref, tnk_ref,
                 mq_ref, mkv_ref, nq_ref, nk_ref, nv_ref, ncmp_ref, gate_ref):
    h = _rms(x_ref[0], g_ref[...])
    p = _dot(h.astype(BF16), win_ref[...])

    ql = _rms(p[:, C_QLAT:C_QLAT + MLA_Q_RANK], qn_ref[...])
    q = _dot(ql.astype(BF16), wuq_ref[...])
    scale = MLA_QK_DIM ** -0.5 * LOG2E
    for hh in range(MLA_HEADS):
        blk = q[:, hh * LANES:(hh + 1) * LANES]
        mq_ref[0, :, hh * LANES:(hh + 1) * LANES] = (
            _rope(blk, tmla_ref, MLA_ROPE_DIM // 2) * scale).astype(BF16)

    kpe = _rope(p[:, C_KPE:C_KPE + LANES], tmla_ref, MLA_ROPE_DIM // 2)
    kvl = _rms(p[:, C_KVLAT:C_KVLAT + MLA_KV_RANK], kvn_ref[...])
    kv = _dot(kvl.astype(BF16), wukv_ref[...])
    ones_hi = jnp.where(lax.broadcasted_iota(jnp.int32, kpe.shape, 1) >= MLA_V_DIM, 1.0, 0.0)
    for hh in range(MLA_HEADS):
        c0 = hh * 2 * LANES
        mkv_ref[0, :, c0:c0 + LANES] = (kv[:, c0:c0 + LANES] + kpe).astype(BF16)
        mkv_ref[0, :, c0 + LANES:c0 + 2 * LANES] = (kv[:, c0 + LANES:c0 + 2 * LANES] + ones_hi).astype(BF16)

    nscale = NSA_HEAD_DIM ** -0.5 * LOG2E
    for j in range(NSA_HEADS * NSA_HEAD_DIM // LANES):
        blk = p[:, C_NQ + j * LANES:C_NQ + (j + 1) * LANES]
        nq_ref[0, :, j * LANES:(j + 1) * LANES] = (
            _rope(blk, tnq_ref, NSA_ROT_DIM // 2) * nscale).astype(BF16)

    low = lax.broadcasted_iota(jnp.int32, kpe.shape, 1) < NSA_HEAD_DIM
    for j in range(NSA_BRANCHES * NSA_KV_HEADS):
        blk = _rope(p[:, C_NKV + j * LANES:C_NKV + (j + 1) * LANES], tnk_ref, NSA_ROT_DIM // 2)
        if j < NSA_KV_HEADS:
            ncmp_ref[0, :, j * LANES:(j + 1) * LANES] = blk
        else:
            jj = j - NSA_KV_HEADS
            nk_ref[0, :, jj * LANES:(jj + 1) * LANES] = jnp.where(low, blk, 0.0).astype(BF16)
            nv_ref[0, :, jj * LANES:(jj + 1) * LANES] = jnp.where(
                low, pltpu.roll(blk, NSA_HEAD_DIM, 1), 1.0).astype(BF16)

    gate_ref[0] = jax.nn.sigmoid(p[:, C_GATE:C_END] + gb_ref[...])


def _proj_call(x, w, tabs):
    b, s, d = x.shape
    tm = TM_PROJ
    const = lambda shape: pl.BlockSpec(shape, lambda i, bb: (0,) * len(shape))
    tab = pl.BlockSpec((3, tm, LANES), lambda i, bb: (0, i, 0))
    row = lambda width: pl.BlockSpec((1, tm, width), lambda i, bb: (bb, i, 0))
    return pl.pallas_call(
        _proj_kernel,
        grid=(s // tm, b),
        in_specs=[row(d), const((1, d)), const((d, C_END)), const((1, MLA_Q_RANK)),
                  const((1, MLA_KV_RANK)), const((MLA_Q_RANK, MLA_HEADS * LANES)),
                  const((MLA_KV_RANK, MLA_HEADS * 2 * LANES)), const((1, NSA_KV_HEADS * LANES)),
                  tab, tab, tab],
        out_specs=[row(MLA_HEADS * LANES), row(MLA_HEADS * 2 * LANES), row(NSA_HEADS * NSA_HEAD_DIM),
                   row(2 * NSA_KV_HEADS * LANES), row(2 * NSA_KV_HEADS * LANES),
                   row(NSA_KV_HEADS * LANES), row(NSA_KV_HEADS * LANES)],
        out_shape=[jax.ShapeDtypeStruct((b, s, MLA_HEADS * LANES), BF16),
                   jax.ShapeDtypeStruct((b, s, MLA_HEADS * 2 * LANES), BF16),
                   jax.ShapeDtypeStruct((b, s, NSA_HEADS * NSA_HEAD_DIM), BF16),
                   jax.ShapeDtypeStruct((b, s, 2 * NSA_KV_HEADS * LANES), BF16),
                   jax.ShapeDtypeStruct((b, s, 2 * NSA_KV_HEADS * LANES), BF16),
                   jax.ShapeDtypeStruct((b, s, NSA_KV_HEADS * LANES), F32),
                   jax.ShapeDtypeStruct((b, s, NSA_KV_HEADS * LANES), F32)],
        compiler_params=pltpu.CompilerParams(
            dimension_semantics=("arbitrary", "arbitrary"), vmem_limit_bytes=VMEM_LIMIT),
        name="proj",
    )(x, w["attn_norm"], w["w_in"], w["q_norm"], w["kv_norm"], w["w_uq"], w["w_ukv"], w["gate_bias"],
      tabs["mla"], tabs["nq"], tabs["nk"])


def _cmp_kernel(x_ref, pos_ref, w1_ref, b1_ref, w2_ref, kc_ref, vc_ref):
    nc = kc_ref.shape[2]
    half = CMP_LEN // 2
    a = jnp.zeros((nc, 2 * CMP_HIDDEN), F32)
    bm = jnp.zeros((nc, 2 * CMP_HIDDEN), F32)
    for l in range(half):
        xl = x_ref[0, pl.ds(l, nc, stride=CMP_STRIDE), :]
        a = a + _dot((xl + pos_ref[l:l + 1, :]).astype(BF16), w1_ref[l])
        bm = bm + _dot((xl + pos_ref[half + l:half + l + 1, :]).astype(BF16), w1_ref[half + l])
    hid = a + pltpu.roll(bm, nc - 1, 0) + b1_ref[...]
    r = _dot(jax.nn.gelu(hid).astype(BF16), w2_ref[...])
    low = lax.broadcasted_iota(jnp.int32, r.shape, 1) < NSA_HEAD_DIM
    kc_ref[0, 0] = jnp.where(low, r, 0.0).astype(BF16)
    vc_ref[0, 0] = jnp.where(low, pltpu.roll(r, NSA_HEAD_DIM, 1), 1.0).astype(BF16)


def _cmp_call(ncmp, w):
    b, s, _ = ncmp.shape
    nc = s // CMP_STRIDE
    const = lambda shape: pl.BlockSpec(shape, lambda bb, g: (0,) * len(shape))
    return pl.pallas_call(
        _cmp_kernel,
        grid=(b, NSA_KV_HEADS),
        in_specs=[pl.BlockSpec((1, s, LANES), lambda bb, g: (bb, 0, g)),
                  const((CMP_LEN, LANES)), const((CMP_LEN, LANES, 2 * CMP_HIDDEN)),
                  const((1, 2 * CMP_HIDDEN)), const((2 * CMP_HIDDEN, LANES))],
        out_specs=[pl.BlockSpec((1, 1, nc, LANES), lambda bb, g: (bb, g, 0, 0))] * 2,
        out_shape=[jax.ShapeDtypeStruct((b, NSA_KV_HEADS, nc, LANES), BF16)] * 2,
        compiler_params=pltpu.CompilerParams(
            dimension_semantics=("arbitrary", "arbitrary"), vmem_limit_bytes=VMEM_LIMIT),
        name="compress",
    )(ncmp, w["cmp_pos"], w["cmp_w1"], w["cmp_b1"], w["cmp_w2"])


def _mla_kernel(q_ref, kv_ref, o_ref, m_ref, acc_ref):
    tq = q_ref.shape[1]
    tk = tq
    qi = pl.program_id(2)
    m_ref[...] = jnp.full(m_ref.shape, NEG_INF, F32)
    acc_ref[...] = jnp.zeros(acc_ref.shape, F32)

    def step(j, masked):
        k0 = pl.multiple_of(j * tk, tk)
        for hh in range(MLA_HEADS_PER_STEP):
            cols = slice(hh * LANES, (hh + 1) * LANES)
            k = kv_ref[0, pl.ds(k0, tk), 2 * hh * LANES:(2 * hh + 1) * LANES]
            v = kv_ref[0, pl.ds(k0, tk), (2 * hh + 1) * LANES:(2 * hh + 2) * LANES]
            s = _dot_nt(q_ref[0, :, cols], k)
            if masked:
                s = jnp.where(lax.broadcasted_iota(jnp.int32, s.shape, 1)
                              <= lax.broadcasted_iota(jnp.int32, s.shape, 0), s, NEG_INF)
            m_old = m_ref[:, cols]
            m_new = jnp.maximum(m_old, jnp.max(s, axis=1, keepdims=True))
            p = jnp.exp2(s - jnp.tile(m_new, (1, tk // LANES)))
            acc_ref[:, cols] = jnp.exp2(m_old - m_new) * acc_ref[:, cols] + _dot(p.astype(BF16), v)
            m_ref[:, cols] = m_new

    def body(j, c):
        step(j, False)
        return c

    lax.fori_loop(0, qi, body, 0)
    step(qi, True)
    for hh in range(MLA_HEADS_PER_STEP):
        cols = slice(hh * LANES, (hh + 1) * LANES)
        acc = acc_ref[:, cols]
        lane = lax.broadcasted_iota(jnp.int32, acc.shape, 1)
        denom = jnp.where(lane < MLA_V_DIM, pltpu.roll(acc, MLA_V_DIM, 1), acc)
        o_ref[0, :, cols] = (acc / denom).astype(BF16)


def _mla_call(mq, mkv):
    b, s, _ = mq.shape
    tq = TQ_MLA
    hp = MLA_HEADS_PER_STEP
    return pl.pallas_call(
        _mla_kernel,
        grid=(b, MLA_HEADS // hp, s // tq),
        in_specs=[pl.BlockSpec((1, tq, hp * LANES), lambda bb, h, i: (bb, i, h)),
                  pl.BlockSpec((1, s, 2 * hp * LANES), lambda bb, h, i: (bb, 0, h))],
        out_specs=pl.BlockSpec((1, tq, hp * LANES), lambda bb, h, i: (bb, i, h)),
        out_shape=jax.ShapeDtypeStruct((b, s, MLA_HEADS * LANES), BF16),
        scratch_shapes=[pltpu.VMEM((tq, hp * LANES), F32), pltpu.VMEM((tq, hp * LANES), F32)],
        compiler_params=pltpu.CompilerParams(
            dimension_semantics=("arbitrary", "arbitrary", "arbitrary"), vmem_limit_bytes=VMEM_LIMIT),
        name="mla_attn",
    )(mq, mkv)


def _nsa_kernel(q_ref, kc_ref, vc_ref, ks_ref, vs_ref, kw_ref, vw_ref, gate_ref, ovt_ref, ext_ref, o_ref,
                m_ref, acc_ref, *, seq):
    tq = q_ref.shape[1]
    hg = NSA_GROUP
    nc = kc_ref.shape[2]
    tk = TK_SLC
    q0 = pl.program_id(2) * tq

    xq = q_ref[0].astype(F32)
    parts = []
    for j in range(hg * NSA_HEAD_DIM // LANES):
        a = xq[:, j * LANES:(j + 1) * LANES]
        parts.append(a)
        parts.append(pltpu.roll(a, NSA_HEAD_DIM, 1))
    qst = jnp.concatenate(parts, axis=0).astype(BF16)

    def normalized(acc):
        lane = lax.broadcasted_iota(jnp.int32, acc.shape, 1)
        return acc / jnp.where(lane < NSA_HEAD_DIM, pltpu.roll(acc, NSA_HEAD_DIM, 1), acc)

    kc = kc_ref[0, 0]
    vc = vc_ref[0, 0]
    sc = _dot_nt(qst, kc)
    qpos_c = q0 + lax.broadcasted_iota(jnp.int32, (tq, nc), 0)
    cend = lax.broadcasted_iota(jnp.int32, (tq, nc), 1) * CMP_STRIDE + (CMP_LEN - 1)
    mask_c = cend <= qpos_c
    pcs = jnp.zeros((tq, nc), F32)
    o_cmp = []
    for h in range(hg):
        sh = jnp.where(mask_c, sc[h * tq:(h + 1) * tq], NEG_INF)
        e = jnp.exp2(sh - jnp.max(sh, axis=1, keepdims=True))
        pc = jnp.where(mask_c, e, 0.0) * (1.0 / jnp.sum(e, axis=1, keepdims=True))
        pcs = pcs + pc
        o_cmp.append(_dot(pc.astype(BF16), vc))
    ovt = ovt_ref[...]
    p_hi = pcs.astype(BF16)
    r1 = pcs - p_hi.astype(F32)
    p_mid = r1.astype(BF16)
    p_lo = (r1 - p_mid.astype(F32)).astype(BF16)
    imp = _dot_nt(ovt, p_hi) + _dot_nt(ovt, p_mid) + _dot_nt(ovt, p_lo)

    blk = lax.broadcasted_iota(jnp.int32, (LANES, tq), 0)
    cur = (q0 + lax.broadcasted_iota(jnp.int32, (LANES, tq), 1)) // SLC_LEN
    forced = (blk == 0) | (blk == cur) | (blk == cur - 1)
    allowed = blk <= cur
    wl = WINDOW + tq
    start = pl.multiple_of(jnp.maximum(q0 - WINDOW, 0), tq)
    sw = _dot_nt(qst, kw_ref[0, pl.ds(start, wl), :])
    vw = vw_ref[0, pl.ds(start, wl), :]
    qpos_w = q0 + lax.broadcasted_iota(jnp.int32, (tq, wl), 0)
    kpos_w = start + lax.broadcasted_iota(jnp.int32, (tq, wl), 1)
    mask_w = jnp.where(kpos_w <= qpos_w, kpos_w, -WINDOW - seq) > qpos_w - WINDOW
    o_win = []
    for h in range(hg):
        sh = jnp.where(mask_w, sw[h * tq:(h + 1) * tq], NEG_INF)
        e = jnp.exp2(sh - jnp.max(sh, axis=1, keepdims=True))
        o_win.append(normalized(_dot(e.astype(BF16), vw)))

    score = jnp.where(forced, -2.0, jnp.where(allowed, imp, -1.0))
    blk_f = blk.astype(F32)
    bias = jnp.where(forced, 0.0, NEG_INF)
    for _ in range(SLC_TOPK - 3):
        mx = jnp.max(score, axis=0, keepdims=True)
        first = jnp.min(jnp.where(score == mx, blk_f, float(LANES)), axis=0, keepdims=True)
        hit = blk_f == first
        bias = jnp.where(hit, 0.0, bias)
        score = jnp.where(hit, -2.0, score)
    selneg = jnp.where(allowed, bias, NEG_INF).T.astype(BF16)
    qaug = jnp.concatenate([qst, jnp.concatenate([selneg] * hg, axis=0)], axis=1)

    m_ref[...] = jnp.full(m_ref.shape, NEG_INF, F32)
    acc_ref[...] = jnp.zeros(acc_ref.shape, F32)

    def slc_step(j, masked):
        k0 = pl.multiple_of(j * tk, tk)
        kaug = jnp.concatenate([ks_ref[0, pl.ds(k0, tk), :], ext_ref[pl.ds(k0, tk), :]], axis=1)
        s = _dot_nt(qaug, kaug)
        v = vs_ref[0, pl.ds(k0, tk), :]
        if masked:
            causal = (k0 + lax.broadcasted_iota(jnp.int32, (tq, tk), 1)
                      <= q0 + lax.broadcasted_iota(jnp.int32, (tq, tk), 0))
        for h in range(hg):
            rows = slice(h * tq, (h + 1) * tq)
            sh = s[rows]
            if masked:
                sh = jnp.where(causal, sh, NEG_INF)
            m_old = m_ref[rows]
            m_new = jnp.maximum(m_old, jnp.max(sh, axis=1, keepdims=True))
            p = jnp.exp2(sh - jnp.tile(m_new, (1, tk // LANES)))
            acc_ref[rows] = jnp.exp2(m_old - m_new) * acc_ref[rows] + _dot(p.astype(BF16), v)
            m_ref[rows] = m_new

    def slc_body(j, c):
        slc_step(j, False)
        return c

    n_full = q0 // tk
    lax.fori_loop(0, n_full, slc_body, 0)
    slc_step(n_full, True)

    gates = gate_ref[0]
    for h in range(hg):
        o_slc = normalized(acc_ref[h * tq:(h + 1) * tq])
        c = h * NSA_BRANCHES
        o = gates[:, c:c + 1] * o_cmp[h] + gates[:, c + 1:c + 2] * o_slc + gates[:, c + 2:c + 3] * o_win[h]
        o_ref[0, :, h * LANES:(h + 1) * LANES] = o.astype(BF16)


def _nsa_call(nq, kc, vc, nk, nv, gates, ov, ext):
    b, s, _ = nq.shape
    tq = TQ_NSA
    hg = NSA_GROUP
    nc = kc.shape[2]
    cmp_spec = pl.BlockSpec((1, 1, nc, LANES), lambda bb, g, i: (bb, g, 0, 0))
    slc_spec = pl.BlockSpec((1, s, LANES), lambda bb, g, i: (bb, 0, g))
    win_spec = pl.BlockSpec((1, s, LANES), lambda bb, g, i: (bb, 0, NSA_KV_HEADS + g))
    return pl.pallas_call(
        functools.partial(_nsa_kernel, seq=s),
        grid=(b, NSA_KV_HEADS, s // tq),
        in_specs=[pl.BlockSpec((1, tq, hg * NSA_HEAD_DIM), lambda bb, g, i: (bb, i, g)),
                  cmp_spec, cmp_spec, slc_spec, slc_spec, win_spec, win_spec,
                  pl.BlockSpec((1, tq, LANES), lambda bb, g, i: (bb, i, g)),
                  pl.BlockSpec((LANES, nc), lambda bb, g, i: (0, 0)),
                  pl.BlockSpec((s, LANES), lambda bb, g, i: (0, 0))],
        out_specs=pl.BlockSpec((1, tq, hg * LANES), lambda bb, g, i: (bb, i, g)),
        out_shape=jax.ShapeDtypeStruct((b, s, NSA_HEADS * LANES), BF16),
        scratch_shapes=[pltpu.VMEM((hg * tq, LANES), F32), pltpu.VMEM((hg * tq, LANES), F32)],
        compiler_params=pltpu.CompilerParams(
            dimension_semantics=("arbitrary", "arbitrary", "arbitrary"), vmem_limit_bytes=VMEM_LIMIT),
        name="nsa_attn",
    )(nq, kc, vc, nk, nv, nk, nv, gates, ov, ext)


def _ffn_kernel(x_ref, xh_ref, am_ref, amh_ref, an_ref, anh_ref, wom_ref, won_ref, g_ref,
                wu_ref, wv_ref, cwu_ref, cwv_ref, cbu_ref, cbv_ref, wd_ref, fn_ref, o_ref,
                x1_ref, hn_ref, acc_ref, *, final):
    tm = x_ref.shape[1]
    i = pl.program_id(1)
    f = pl.program_id(2)

    @pl.when(f == 0)
    def _():
        def piece(xv, am, an):
            x1 = xv + _dot(am, wom_ref[...]) + _dot(an, won_ref[...])
            return x1, _rms(x1, g_ref[...]).astype(BF16)

        x1, hn = piece(x_ref[0], am_ref[0], an_ref[0])
        x1_ref[...] = x1
        hn_ref[HALO:, :] = hn
        _, hnh = piece(xh_ref[0], amh_ref[0], anh_ref[0])
        hn_ref[0:HALO, :] = jnp.where(i > 0, hnh, jnp.zeros_like(hnh))
        acc_ref[...] = jnp.zeros(acc_ref.shape, F32)

    hn = hn_ref[...]

    def conv(w_ref, cw_ref, cb_ref):
        hh = _dot(hn, w_ref[...])
        return (hh[HALO:] * cw_ref[2:3, :] + hh[HALO - 1:HALO - 1 + tm] * cw_ref[1:2, :]
                + hh[HALO - 2:HALO - 2 + tm] * cw_ref[0:1, :] + cb_ref[...])

    u = conv(wu_ref, cwu_ref, cbu_ref)
    v = conv(wv_ref, cwv_ref, cbv_ref)
    acc_ref[...] += _dot((u * jax.nn.sigmoid(u) * v).astype(BF16), wd_ref[...])

    @pl.when(f == pl.num_programs(2) - 1)
    def _():
        y = x1_ref[...] + acc_ref[...]
        if final:
            y = _rms(y, fn_ref[...])
        o_ref[0] = y


def _ffn_call(x, am, an, w, final_norm, final):
    b, s, d = x.shape
    tm, tf = TM_FFN, TF_FFN
    nf = D_FF // tf
    hb = tm // HALO
    const = lambda shape: pl.BlockSpec(shape, lambda bb, i, f: (0,) * len(shape))
    row = lambda width: pl.BlockSpec((1, tm, width), lambda bb, i, f: (bb, i, 0))
    halo = lambda width: pl.BlockSpec((1, HALO, width), lambda bb, i, f: (bb, jnp.maximum(i * hb - 1, 0), 0))
    na = am.shape[2]
    return pl.pallas_call(
        functools.partial(_ffn_kernel, final=final),
        grid=(b, s // tm, nf),
        in_specs=[row(d), halo(d), row(na), halo(na), row(na), halo(na),
                  const((na, d)), const((na, d)), const((1, d)),
                  pl.BlockSpec((d, tf), lambda bb, i, f: (0, f)),
                  pl.BlockSpec((d, tf), lambda bb, i, f: (0, nf + f)),
                  pl.BlockSpec((CONV_WIDTH, tf), lambda bb, i, f: (0, f)),
                  pl.BlockSpec((CONV_WIDTH, tf), lambda bb, i, f: (0, nf + f)),
                  pl.BlockSpec((1, tf), lambda bb, i, f: (0, f)),
                  pl.BlockSpec((1, tf), lambda bb, i, f: (0, nf + f)),
                  pl.BlockSpec((tf, d), lambda bb, i, f: (f, 0)),
                  const((1, d))],
        out_specs=row(d),
        out_shape=jax.ShapeDtypeStruct((b, s, d), F32),
        scratch_shapes=[pltpu.VMEM((tm, d), F32), pltpu.VMEM((HALO + tm, d), BF16), pltpu.VMEM((tm, d), F32)],
        compiler_params=pltpu.CompilerParams(
            dimension_semantics=("arbitrary", "arbitrary", "arbitrary"), vmem_limit_bytes=VMEM_LIMIT),
        name="out_ffn",
    )(x, x, am, am, an, an, w["wo_m"], w["wo_n"], w["ffn_norm"], w["w_up"], w["w_up"],
      w["conv_w"], w["conv_w"], w["conv_b"], w["conv_b"], w["w_down"], final_norm)


def _rope_tables(seq):
    def cs(dim):
        inv = ROPE_THETA ** (-jnp.arange(0, dim, 2, dtype=F32) / dim)
        ang = jnp.arange(seq, dtype=F32)[:, None] * inv[None, :]
        return jnp.cos(ang), jnp.sin(ang)

    def table(cos, sin, offsets):
        half = cos.shape[1]
        zero = jnp.zeros_like(sin)
        c, sa, sb, at = [], [], [], 0
        for o in offsets:
            gap = (seq, o - at)
            c += [jnp.ones(gap, F32), cos, cos]
            sa += [jnp.zeros(gap, F32), -sin, zero]
            sb += [jnp.zeros(gap, F32), zero, sin]
            at = o + 2 * half
        tail = (seq, LANES - at)
        cat = lambda parts, fill: jnp.concatenate(parts + [jnp.full(tail, fill, F32)], axis=1)
        return jnp.stack([cat(c, 1.0), cat(sa, 0.0), cat(sb, 0.0)])

    cm, sm = cs(MLA_ROPE_DIM)
    cn, sn = cs(NSA_ROT_DIM)
    return {"mla": table(cm, sm, (MLA_NOPE_DIM,)),
            "nq": table(cn, sn, (0, NSA_HEAD_DIM)),
            "nk": table(cn, sn, (0,))}


def _select_tables(seq):
    nc = seq // CMP_STRIDE
    n_c = (seq - CMP_LEN) // CMP_STRIDE + 1
    n_s = seq // SLC_LEN
    c = np.arange(nc)[:, None]
    sb = np.arange(LANES)[None, :]
    cs, ce = c * CMP_STRIDE, c * CMP_STRIDE + CMP_LEN - 1
    ss = sb * SLC_LEN
    ov = (cs <= ss + SLC_LEN - 1) & (ce >= ss) & (c < n_c) & (sb < n_s)
    ext = (np.arange(seq)[:, None] // SLC_LEN) == np.arange(LANES)[None, :]
    return jnp.asarray(ov.T, BF16), jnp.asarray(ext, BF16)


def _layer_weights(l, attn_norm, w_in, mla_q_norm, mla_kv_norm, mla_w_uq, mla_w_ukv, nsa_gate_bias,
                   nsa_cmp_pos, nsa_cmp_w1, nsa_cmp_b1, nsa_cmp_w2, w_out, ffn_norm, ffn_w_up,
                   ffn_conv_w, ffn_conv_b, ffn_w_down):
    d = D_MODEL
    wi = w_in[l]
    z = lambda n: jnp.zeros((d, n), F32)
    kpe = jnp.concatenate([z(MLA_NOPE_DIM), wi[:, _OFF[2]:_OFF[3]], z(LANES - MLA_QK_DIM)], axis=1)
    nkv = wi[:, _OFF[4]:_OFF[5]].reshape(d, NSA_BRANCHES, 2, NSA_KV_HEADS, NSA_HEAD_DIM)
    nkv = nkv.transpose(0, 1, 3, 2, 4).reshape(d, -1)
    gpg = NSA_GROUP * NSA_BRANCHES
    gw = wi[:, _OFF[5]:_OFF[6]].reshape(d, NSA_KV_HEADS, gpg)
    gw = jnp.pad(gw, ((0, 0), (0, 0), (0, LANES - gpg))).reshape(d, -1)
    w_in_r = jnp.concatenate([wi[:, _OFF[0]:_OFF[2]], kpe, wi[:, _OFF[3]:_OFF[4]], nkv, gw], axis=1)
    gb = jnp.pad(nsa_gate_bias[l].reshape(NSA_KV_HEADS, gpg), ((0, 0), (0, LANES - gpg))).reshape(1, -1)

    wuq = mla_w_uq[l].reshape(MLA_Q_RANK, MLA_HEADS, MLA_QK_DIM)
    wuq = jnp.pad(wuq, ((0, 0), (0, 0), (0, LANES - MLA_QK_DIM))).reshape(MLA_Q_RANK, -1)
    wukv = mla_w_ukv[l].reshape(MLA_KV_RANK, MLA_HEADS, 2, MLA_NOPE_DIM)
    wukv = jnp.pad(wukv, ((0, 0), (0, 0), (0, 0), (0, LANES - MLA_NOPE_DIM))).reshape(MLA_KV_RANK, -1)

    pos = jnp.concatenate([nsa_cmp_pos[l, 0], nsa_cmp_pos[l, 1]], axis=1)
    w1 = nsa_cmp_w1[l].reshape(2, CMP_LEN, NSA_HEAD_DIM, CMP_HIDDEN)
    zz = jnp.zeros_like(w1[0])
    w1c = jnp.concatenate([jnp.concatenate([w1[0], zz], axis=2), jnp.concatenate([zz, w1[1]], axis=2)], axis=1)
    b1c = nsa_cmp_b1[l].reshape(1, -1)
    w2 = nsa_cmp_w2[l]
    z2 = jnp.zeros_like(w2[0])
    w2c = jnp.concatenate([jnp.concatenate([w2[0], z2], axis=1), jnp.concatenate([z2, w2[1]], axis=1)], axis=0)

    wo = w_out[l]
    nm = MLA_HEADS * MLA_V_DIM
    wo_m = jnp.pad(wo[:nm].reshape(MLA_HEADS, MLA_V_DIM, d), ((0, 0), (0, LANES - MLA_V_DIM), (0, 0)))
    wo_n = jnp.pad(wo[nm:].reshape(NSA_HEADS, NSA_HEAD_DIM, d), ((0, 0), (0, LANES - NSA_HEAD_DIM), (0, 0)))
    return {
        "attn_norm": attn_norm[l].reshape(1, d), "w_in": w_in_r.astype(BF16),
        "q_norm": mla_q_norm[l].reshape(1, -1), "kv_norm": mla_kv_norm[l].reshape(1, -1),
        "w_uq": wuq.astype(BF16), "w_ukv": wukv.astype(BF16), "gate_bias": gb,
        "cmp_pos": pos, "cmp_w1": w1c.astype(BF16), "cmp_b1": b1c, "cmp_w2": w2c.astype(BF16),
        "wo_m": wo_m.reshape(-1, d).astype(BF16), "wo_n": wo_n.reshape(-1, d).astype(BF16),
        "ffn_norm": ffn_norm[l].reshape(1, d), "w_up": ffn_w_up[l].astype(BF16),
        "conv_w": ffn_conv_w[l], "conv_b": ffn_conv_b[l].reshape(1, -1), "w_down": ffn_w_down[l].astype(BF16),
    }


def kernel(x, attn_norm, w_in, mla_q_norm, mla_kv_norm, mla_w_uq, mla_w_ukv, nsa_gate_bias, nsa_cmp_pos, nsa_cmp_w1, nsa_cmp_b1, nsa_cmp_w2, w_out, ffn_norm, ffn_w_up, ffn_conv_w, ffn_conv_b, ffn_w_down, final_norm):
    b, s, d = x.shape
    depth = w_in.shape[0]
    assert d == D_MODEL and s % TQ_MLA == 0 and s >= WINDOW + TQ_NSA and s // SLC_LEN <= LANES
    tabs = _rope_tables(s)
    ov, ext = _select_tables(s)
    fn = final_norm.reshape(1, d)
    for l in range(depth):
        w = _layer_weights(l, attn_norm, w_in, mla_q_norm, mla_kv_norm, mla_w_uq, mla_w_ukv, nsa_gate_bias,
                           nsa_cmp_pos, nsa_cmp_w1, nsa_cmp_b1, nsa_cmp_w2, w_out, ffn_norm, ffn_w_up,
                           ffn_conv_w, ffn_conv_b, ffn_w_down)
        mq, mkv, nq, nk, nv, ncmp, gates = _proj_call(x, w, tabs)
        kc, vc = _cmp_call(ncmp, w)
        o_mla = _mla_call(mq, mkv)
        o_nsa = _nsa_call(nq, kc, vc, nk, nv, gates, ov, ext)
        x = _ffn_call(x, o_mla, o_nsa, w, fn, final=(l == depth - 1))
    return x
```

```python
import functools

import jax
import jax.numpy as jnp
import numpy as np
from jax import lax
from jax.experimental import pallas as pl
from jax.experimental.pallas import tpu as pltpu

F32 = jnp.float32
BF16 = jnp.bfloat16

LANES = 128

D_MODEL = 1024
ROPE_THETA = 500000.0
NORM_EPS = 1e-6
NEG_INF = -1e30
LOG2E = 1.4426950408889634

MLA_HEADS = 8
MLA_Q_RANK = 256
MLA_KV_RANK = 128
MLA_NOPE_DIM = 64
MLA_ROPE_DIM = 32
MLA_V_DIM = 64
MLA_QK_DIM = MLA_NOPE_DIM + MLA_ROPE_DIM

NSA_HEADS = 8
NSA_KV_HEADS = 2
NSA_GROUP = NSA_HEADS // NSA_KV_HEADS
NSA_HEAD_DIM = 64
NSA_ROT_DIM = NSA_HEAD_DIM // 4
NSA_BRANCHES = 3
CMP_LEN = 32
CMP_STRIDE = 16
CMP_HIDDEN = 128
SLC_LEN = 64
SLC_TOPK = 16
WINDOW = 512
FORCE_SCORE = 1e6

D_FF = 2816
CONV_WIDTH = 3

IN_SPLITS = (MLA_Q_RANK, MLA_KV_RANK, MLA_ROPE_DIM, NSA_HEADS * NSA_HEAD_DIM,
             NSA_BRANCHES * 2 * NSA_KV_HEADS * NSA_HEAD_DIM, NSA_HEADS * NSA_BRANCHES)
_OFF = tuple(int(v) for v in np.cumsum((0,) + IN_SPLITS))

C_QLAT = 0
C_KVLAT = C_QLAT + MLA_Q_RANK
C_KPE = C_KVLAT + MLA_KV_RANK
C_NQ = C_KPE + LANES
C_NKV = C_NQ + NSA_HEADS * NSA_HEAD_DIM
C_GATE = C_NKV + NSA_BRANCHES * NSA_KV_HEADS * LANES
C_END = C_GATE + NSA_KV_HEADS * LANES

VMEM_LIMIT = 56 * 1024 * 1024

TM_PROJ = 512
TQ_MLA = 512
MLA_HEADS_PER_STEP = 4
TQ_NSA = 256
TK_SLC = 512
TM_FFN = 512
TF_FFN = 1408
HALO = 16


def _rms(x, g):
    return x * lax.rsqrt(jnp.mean(x * x, axis=-1, keepdims=True) + NORM_EPS) * g


def _rope(x, tab_ref, half):
    return (x * tab_ref[0] + pltpu.roll(x, LANES - half, 1) * tab_ref[1]
            + pltpu.roll(x, half, 1) * tab_ref[2])


def _dot(a, b):
    return jnp.dot(a, b, preferred_element_type=F32)


def _dot_nt(a, b):
    return lax.dot_general(a, b, (((1,), (1,)), ((), ())), preferred_element_type=F32)


def _proj_kernel(x_ref, g_ref, win_ref, qn_ref, kvn_ref, wuq_ref, wukv_ref, gb_ref,
                 tmla_ref, tnq_ref, tnk_ref,
                 mq_ref, mkv_ref, nq_ref, nk_ref, nv_ref, ncmp_ref, gate_ref):
    h = _rms(x_ref[0], g_ref[...])
    p = _dot(h.astype(BF16), win_ref[...])

    ql = _rms(p[:, C_QLAT:C_QLAT + MLA_Q_RANK], qn_ref[...])
    q = _dot(ql.astype(BF16), wuq_ref[...])
    scale = MLA_QK_DIM ** -0.5 * LOG2E
    for hh in range(MLA_HEADS):
        blk = q[:, hh * LANES:(hh + 1) * LANES]
        mq_ref[0, :, hh * LANES:(hh + 1) * LANES] = (
            _rope(blk, tmla_ref, MLA_ROPE_DIM // 2) * scale).astype(BF16)

    kpe = _rope(p[:, C_KPE:C_KPE + LANES], tmla_ref, MLA_ROPE_DIM // 2)
    kvl = _rms(p[:, C_KVLAT:C_KVLAT + MLA_KV_RANK], kvn_ref[...])
    kv = _dot(kvl.astype(BF16), wukv_ref[...])
    ones_hi = jnp.where(lax.broadcasted_iota(jnp.int32, kpe.shape, 1) >= MLA_V_DIM, 1.0, 0.0)
    for hh in range(MLA_HEADS):
        c0 = hh * 2 * LANES
        mkv_ref[0, :, c0:c0 + LANES] = (kv[:, c0:c0 + LANES] + kpe).astype(BF16)
        mkv_ref[0, :, c0 + LANES:c0 + 2 * LANES] = (kv[:, c0 + LANES:c0 + 2 * LANES] + ones_hi).astype(BF16)

    nscale = NSA_HEAD_DIM ** -0.5 * LOG2E
    for j in range(NSA_HEADS * NSA_HEAD_DIM // LANES):
        blk = p[:, C_NQ + j * LANES:C_NQ + (j + 1) * LANES]
        nq_ref[0, :, j * LANES:(j + 1) * LANES] = (
            _rope(blk, tnq_ref, NSA_ROT_DIM // 2) * nscale).astype(BF16)

    low = lax.broadcasted_iota(jnp.int32, kpe.shape, 1) < NSA_HEAD_DIM
    for j in range(NSA_BRANCHES * NSA_KV_HEADS):
        blk = _rope(p[:, C_NKV + j * LANES:C_NKV + (j + 1) * LANES], tnk_ref, NSA_ROT_DIM // 2)
        if j < NSA_KV_HEADS:
            ncmp_ref[0, :, j * LANES:(j + 1) * LANES] = blk
        else:
            jj = j - NSA_KV_HEADS
            nk_ref[0, :, jj * LANES:(jj + 1) * LANES] = jnp.where(low, blk, 0.0).astype(BF16)
            nv_ref[0, :, jj * LANES:(jj + 1) * LANES] = jnp.where(
                low, pltpu.roll(blk, NSA_HEAD_DIM, 1), 1.0).astype(BF16)

    gate_ref[0] = jax.nn.sigmoid(p[:, C_GATE:C_END] + gb_ref[...])


def _proj_call(x, w, tabs):
    b, s, d = x.shape
    tm = TM_PROJ
    const = lambda shape: pl.BlockSpec(shape, lambda i, bb: (0,) * len(shape))
    tab = pl.BlockSpec((3, tm, LANES), lambda i, bb: (0, i, 0))
    row = lambda width: pl.BlockSpec((1, tm, width), lambda i, bb: (bb, i, 0))
    return pl.pallas_call(
        _proj_kernel,
        grid=(s // tm, b),
        in_specs=[row(d), const((1, d)), const((d, C_END)), const((1, MLA_Q_RANK)),
                  const((1, MLA_KV_RANK)), const((MLA_Q_RANK, MLA_HEADS * LANES)),
                  const((MLA_KV_RANK, MLA_HEADS * 2 * LANES)), const((1, NSA_KV_HEADS * LANES)),
                  tab, tab, tab],
        out_specs=[row(MLA_HEADS * LANES), row(MLA_HEADS * 2 * LANES), row(NSA_HEADS * NSA_HEAD_DIM),
                   row(2 * NSA_KV_HEADS * LANES), row(2 * NSA_KV_HEADS * LANES),
                   row(NSA_KV_HEADS * LANES), row(NSA_KV_HEADS * LANES)],
        out_shape=[jax.ShapeDtypeStruct((b, s, MLA_HEADS * LANES), BF16),
                   jax.ShapeDtypeStruct((b, s, MLA_HEADS * 2 * LANES), BF16),
                   jax.ShapeDtypeStruct((b, s, NSA_HEADS * NSA_HEAD_DIM), BF16),
                   jax.ShapeDtypeStruct((b, s, 2 * NSA_KV_HEADS * LANES), BF16),
                   jax.ShapeDtypeStruct((b, s, 2 * NSA_KV_HEADS * LANES), BF16),
                   jax.ShapeDtypeStruct((b, s, NSA_KV_HEADS * LANES), F32),
                   jax.ShapeDtypeStruct((b, s, NSA_KV_HEADS * LANES), F32)],
        compiler_params=pltpu.CompilerParams(
            dimension_semantics=("arbitrary", "arbitrary"), vmem_limit_bytes=VMEM_LIMIT),
        name="proj",
    )(x, w["attn_norm"], w["w_in"], w["q_norm"], w["kv_norm"], w["w_uq"], w["w_ukv"], w["gate_bias"],
      tabs["mla"], tabs["nq"], tabs["nk"])


def _cmp_kernel(x_ref, pos_ref, w1_ref, b1_ref, w2_ref, kc_ref, vc_ref):
    nc = kc_ref.shape[2]
    half = CMP_LEN // 2
    a = jnp.zeros((nc, 2 * CMP_HIDDEN), F32)
    bm = jnp.zeros((nc, 2 * CMP_HIDDEN), F32)
    for l in range(half):
        xl = x_ref[0, pl.ds(l, nc, stride=CMP_STRIDE), :]
        a = a + _dot((xl + pos_ref[l:l + 1, :]).astype(BF16), w1_ref[l])
        bm = bm + _dot((xl + pos_ref[half + l:half + l + 1, :]).astype(BF16), w1_ref[half + l])
    hid = a + pltpu.roll(bm, nc - 1, 0) + b1_ref[...]
    r = _dot(jax.nn.gelu(hid).astype(BF16), w2_ref[...])
    low = lax.broadcasted_iota(jnp.int32, r.shape, 1) < NSA_HEAD_DIM
    kc_ref[0, 0] = jnp.where(low, r, 0.0).astype(BF16)
    vc_ref[0, 0] = jnp.where(low, pltpu.roll(r, NSA_HEAD_DIM, 1), 1.0).astype(BF16)


def _cmp_call(ncmp, w):
    b, s, _ = ncmp.shape
    nc = s // CMP_STRIDE
    const = lambda shape: pl.BlockSpec(shape, lambda bb, g: (0,) * len(shape))
    return pl.pallas_call(
        _cmp_kernel,
        grid=(b, NSA_KV_HEADS),
        in_specs=[pl.BlockSpec((1, s, LANES), lambda bb, g: (bb, 0, g)),
                  const((CMP_LEN, LANES)), const((CMP_LEN, LANES, 2 * CMP_HIDDEN)),
                  const((1, 2 * CMP_HIDDEN)), const((2 * CMP_HIDDEN, LANES))],
        out_specs=[pl.BlockSpec((1, 1, nc, LANES), lambda bb, g: (bb, g, 0, 0))] * 2,
        out_shape=[jax.ShapeDtypeStruct((b, NSA_KV_HEADS, nc, LANES), BF16)] * 2,
        compiler_params=pltpu.CompilerParams(
            dimension_semantics=("arbitrary", "arbitrary"), vmem_limit_bytes=VMEM_LIMIT),
        name="compress",
    )(ncmp, w["cmp_pos"], w["cmp_w1"], w["cmp_b1"], w["cmp_w2"])


def _mla_kernel(q_ref, kv_ref, o_ref, m_ref, acc_ref):
    tq = q_ref.shape[1]
    tk = tq
    qi = pl.program_id(2)
    m_ref[...] = jnp.full(m_ref.shape, NEG_INF, F32)
    acc_ref[...] = jnp.zeros(acc_ref.shape, F32)

    def step(j, masked):
        k0 = pl.multiple_of(j * tk, tk)
        for hh in range(MLA_HEADS_PER_STEP):
            cols = slice(hh * LANES, (hh + 1) * LANES)
            k = kv_ref[0, pl.ds(k0, tk), 2 * hh * LANES:(2 * hh + 1) * LANES]
            v = kv_ref[0, pl.ds(k0, tk), (2 * hh + 1) * LANES:(2 * hh + 2) * LANES]
            s = _dot_nt(q_ref[0, :, cols], k)
            if masked:
                s = jnp.where(lax.broadcasted_iota(jnp.int32, s.shape, 1)
                              <= lax.broadcasted_iota(jnp.int32, s.shape, 0), s, NEG_INF)
            m_old = m_ref[:, cols]
            m_new = jnp.maximum(m_old, jnp.max(s, axis=1, keepdims=True))
            p = jnp.exp2(s - jnp.tile(m_new, (1, tk // LANES)))
            acc_ref[:, cols] = jnp.exp2(m_old - m_new) * acc_ref[:, cols] + _dot(p.astype(BF16), v)
            m_ref[:, cols] = m_new

    def body(jj, c):
        step(2 * jj, False)
        step(2 * jj + 1, False)
        return c

    lax.fori_loop(0, qi // 2, body, 0)

    @pl.when(qi % 2 == 1)
    def _():
        step(qi - 1, False)

    step(qi, True)
    for hh in range(MLA_HEADS_PER_STEP):
        cols = slice(hh * LANES, (hh + 1) * LANES)
        acc = acc_ref[:, cols]
        lane = lax.broadcasted_iota(jnp.int32, acc.shape, 1)
        denom = jnp.where(lane < MLA_V_DIM, pltpu.roll(acc, MLA_V_DIM, 1), acc)
        o_ref[0, :, cols] = (acc / denom).astype(BF16)


def _mla_call(mq, mkv):
    b, s, _ = mq.shape
    tq = TQ_MLA
    hp = MLA_HEADS_PER_STEP
    return pl.pallas_call(
        _mla_kernel,
        grid=(b, MLA_HEADS // hp, s // tq),
        in_specs=[pl.BlockSpec((1, tq, hp * LANES), lambda bb, h, i: (bb, i, h)),
                  pl.BlockSpec((1, s, 2 * hp * LANES), lambda bb, h, i: (bb, 0, h))],
        out_specs=pl.BlockSpec((1, tq, hp * LANES), lambda bb, h, i: (bb, i, h)),
        out_shape=jax.ShapeDtypeStruct((b, s, MLA_HEADS * LANES), BF16),
        scratch_shapes=[pltpu.VMEM((tq, hp * LANES), F32), pltpu.VMEM((tq, hp * LANES), F32)],
        compiler_params=pltpu.CompilerParams(
            dimension_semantics=("arbitrary", "arbitrary", "arbitrary"), vmem_limit_bytes=VMEM_LIMIT),
        name="mla_attn",
    )(mq, mkv)


def _nsa_kernel(q_ref, kc_ref, vc_ref, ks_ref, vs_ref, kw_ref, vw_ref, gate_ref, ovt_ref, ext_ref, o_ref,
                m_ref, acc_ref, *, seq):
    tq = q_ref.shape[1]
    hg = NSA_GROUP
    nc = kc_ref.shape[2]
    tk = TK_SLC
    q0 = pl.program_id(2) * tq

    xq = q_ref[0].astype(F32)
    parts = []
    for j in range(hg * NSA_HEAD_DIM // LANES):
        a = xq[:, j * LANES:(j + 1) * LANES]
        parts.append(a)
        parts.append(pltpu.roll(a, NSA_HEAD_DIM, 1))
    qst = jnp.concatenate(parts, axis=0).astype(BF16)

    def normalized(acc):
        lane = lax.broadcasted_iota(jnp.int32, acc.shape, 1)
        return acc / jnp.where(lane < NSA_HEAD_DIM, pltpu.roll(acc, NSA_HEAD_DIM, 1), acc)

    kc = kc_ref[0, 0]
    vc = vc_ref[0, 0]
    sc = _dot_nt(qst, kc)
    qpos_c = q0 + lax.broadcasted_iota(jnp.int32, (tq, nc), 0)
    cend = lax.broadcasted_iota(jnp.int32, (tq, nc), 1) * CMP_STRIDE + (CMP_LEN - 1)
    mask_c = cend <= qpos_c
    pcs = jnp.zeros((tq, nc), F32)
    o_cmp = []
    lane = lax.broadcasted_iota(jnp.int32, (tq, LANES), 1)
    sees_any = q0 + lax.broadcasted_iota(jnp.int32, (tq, LANES), 0) >= CMP_LEN - 1
    for h in range(hg):
        sh = jnp.where(mask_c, sc[h * tq:(h + 1) * tq], NEG_INF)
        e = jnp.exp2(sh - jnp.max(sh, axis=1, keepdims=True))
        acc = _dot(e.astype(BF16), vc)
        inv = jnp.where(sees_any, 1.0 / jnp.where(lane < NSA_HEAD_DIM, pltpu.roll(acc, NSA_HEAD_DIM, 1), acc), 0.0)
        pcs = pcs + e * jnp.tile(inv, (1, nc // LANES))
        o_cmp.append(acc * inv)
    ovt = ovt_ref[...]
    p_hi = pcs.astype(BF16)
    r1 = pcs - p_hi.astype(F32)
    p_mid = r1.astype(BF16)
    p_lo = (r1 - p_mid.astype(F32)).astype(BF16)
    imp = _dot_nt(ovt, p_hi) + _dot_nt(ovt, p_mid) + _dot_nt(ovt, p_lo)

    blk = lax.broadcasted_iota(jnp.int32, (LANES, tq), 0)
    cur = (q0 + lax.broadcasted_iota(jnp.int32, (LANES, tq), 1)) // SLC_LEN
    forced = (blk == 0) | (blk == cur) | (blk == cur - 1)
    allowed = blk <= cur
    wl = WINDOW + tq
    start = pl.multiple_of(jnp.maximum(q0 - WINDOW, 0), tq)
    sw = _dot_nt(qst, kw_ref[0, pl.ds(start, wl), :])
    vw = vw_ref[0, pl.ds(start, wl), :]
    qpos_w = q0 + lax.broadcasted_iota(jnp.int32, (tq, wl), 0)
    kpos_w = start + lax.broadcasted_iota(jnp.int32, (tq, wl), 1)
    mask_w = jnp.where(kpos_w <= qpos_w, kpos_w, -WINDOW - seq) > qpos_w - WINDOW
    o_win = []
    for h in range(hg):
        sh = jnp.where(mask_w, sw[h * tq:(h + 1) * tq], NEG_INF)
        e = jnp.exp2(sh - jnp.max(sh, axis=1, keepdims=True))
        o_win.append(normalized(_dot(e.astype(BF16), vw)))

    score = jnp.where(forced, -2.0, jnp.where(allowed, imp, -1.0))
    blk_f = blk.astype(F32)
    bias = jnp.where(forced, 0.0, NEG_INF)
    for _ in range(SLC_TOPK - 3):
        mx = jnp.max(score, axis=0, keepdims=True)
        first = jnp.min(jnp.where(score == mx, blk_f, float(LANES)), axis=0, keepdims=True)
        hit = blk_f == first
        bias = jnp.where(hit, 0.0, bias)
        score = jnp.where(hit, -2.0, score)
    selneg = jnp.where(allowed, bias, NEG_INF).T.astype(BF16)
    qaug = jnp.concatenate([qst, jnp.concatenate([selneg] * hg, axis=0)], axis=1)

    m_ref[...] = jnp.full(m_ref.shape, NEG_INF, F32)
    acc_ref[...] = jnp.zeros(acc_ref.shape, F32)

    def slc_step(j, masked):
        k0 = pl.multiple_of(j * tk, tk)
        kaug = jnp.concatenate([ks_ref[0, pl.ds(k0, tk), :], ext_ref[pl.ds(k0, tk), :]], axis=1)
        s = _dot_nt(qaug, kaug)
        v = vs_ref[0, pl.ds(k0, tk), :]
        if masked:
            causal = (k0 + lax.broadcasted_iota(jnp.int32, (tq, tk), 1)
                      <= q0 + lax.broadcasted_iota(jnp.int32, (tq, tk), 0))
        for h in range(hg):
            rows = slice(h * tq, (h + 1) * tq)
            sh = s[rows]
            if masked:
                sh = jnp.where(causal, sh, NEG_INF)
            m_old = m_ref[rows]
            m_new = jnp.maximum(m_old, jnp.max(sh, axis=1, keepdims=True))
            p = jnp.exp2(sh - jnp.tile(m_new, (1, tk // LANES)))
            acc_ref[rows] = jnp.exp2(m_old - m_new) * acc_ref[rows] + _dot(p.astype(BF16), v)
            m_ref[rows] = m_new

    def slc_body(jj, c):
        slc_step(2 * jj, False)
        slc_step(2 * jj + 1, False)
        return c

    n_full = q0 // tk
    lax.fori_loop(0, n_full // 2, slc_body, 0)

    @pl.when(n_full % 2 == 1)
    def _():
        slc_step(n_full - 1, False)

    slc_step(n_full, True)

    gates = gate_ref[0]
    for h in range(hg):
        o_slc = normalized(acc_ref[h * tq:(h + 1) * tq])
        c = h * NSA_BRANCHES
        o = gates[:, c:c + 1] * o_cmp[h] + gates[:, c + 1:c + 2] * o_slc + gates[:, c + 2:c + 3] * o_win[h]
        o_ref[0, :, h * LANES:(h + 1) * LANES] = o.astype(BF16)


def _nsa_call(nq, kc, vc, nk, nv, gates, ov, ext):
    b, s, _ = nq.shape
    tq = TQ_NSA
    hg = NSA_GROUP
    nc = kc.shape[2]
    cmp_spec = pl.BlockSpec((1, 1, nc, LANES), lambda bb, g, i: (bb, g, 0, 0))
    slc_spec = pl.BlockSpec((1, s, LANES), lambda bb, g, i: (bb, 0, g))
    win_spec = pl.BlockSpec((1, s, LANES), lambda bb, g, i: (bb, 0, NSA_KV_HEADS + g))
    return pl.pallas_call(
        functools.partial(_nsa_kernel, seq=s),
        grid=(b, NSA_KV_HEADS, s // tq),
        in_specs=[pl.BlockSpec((1, tq, hg * NSA_HEAD_DIM), lambda bb, g, i: (bb, i, g)),
                  cmp_spec, cmp_spec, slc_spec, slc_spec, win_spec, win_spec,
                  pl.BlockSpec((1, tq, LANES), lambda bb, g, i: (bb, i, g)),
                  pl.BlockSpec((LANES, nc), lambda bb, g, i: (0, 0)),
                  pl.BlockSpec((s, LANES), lambda bb, g, i: (0, 0))],
        out_specs=pl.BlockSpec((1, tq, hg * LANES), lambda bb, g, i: (bb, i, g)),
        out_shape=jax.ShapeDtypeStruct((b, s, NSA_HEADS * LANES), BF16),
        scratch_shapes=[pltpu.VMEM((hg * tq, LANES), F32), pltpu.VMEM((hg * tq, LANES), F32)],
        compiler_params=pltpu.CompilerParams(
            dimension_semantics=("arbitrary", "arbitrary", "arbitrary"), vmem_limit_bytes=VMEM_LIMIT),
        name="nsa_attn",
    )(nq, kc, vc, nk, nv, nk, nv, gates, ov, ext)


def _ffn_kernel(x_ref, xh_ref, am_ref, amh_ref, an_ref, anh_ref, wom_ref, won_ref, g_ref,
                wu_ref, wv_ref, cwu_ref, cwv_ref, cbu_ref, cbv_ref, wd_ref, fn_ref, o_ref,
                x1_ref, hn_ref, acc_ref, *, final):
    tm = x_ref.shape[1]
    i = pl.program_id(1)
    f = pl.program_id(2)

    @pl.when(f == 0)
    def _():
        def piece(xv, am, an):
            x1 = xv + _dot(am, wom_ref[...]) + _dot(an, won_ref[...])
            return x1, _rms(x1, g_ref[...]).astype(BF16)

        x1, hn = piece(x_ref[0], am_ref[0], an_ref[0])
        x1_ref[...] = x1
        hn_ref[HALO:, :] = hn
        _, hnh = piece(xh_ref[0], amh_ref[0], anh_ref[0])
        hn_ref[0:HALO, :] = jnp.where(i > 0, hnh, jnp.zeros_like(hnh))
        acc_ref[...] = jnp.zeros(acc_ref.shape, F32)

    hn = hn_ref[...]

    def conv(w_ref, cw_ref, cb_ref):
        hh = _dot(hn, w_ref[...])
        return (hh[HALO:] * cw_ref[2:3, :] + hh[HALO - 1:HALO - 1 + tm] * cw_ref[1:2, :]
                + hh[HALO - 2:HALO - 2 + tm] * cw_ref[0:1, :] + cb_ref[...])

    u = conv(wu_ref, cwu_ref, cbu_ref)
    v = conv(wv_ref, cwv_ref, cbv_ref)
    acc_ref[...] += _dot((u * jax.nn.sigmoid(u) * v).astype(BF16), wd_ref[...])

    @pl.when(f == pl.num_programs(2) - 1)
    def _():
        y = x1_ref[...] + acc_ref[...]
        if final:
            y = _rms(y, fn_ref[...])
        o_ref[0] = y


def _ffn_call(x, am, an, w, final_norm, final):
    b, s, d = x.shape
    tm, tf = TM_FFN, TF_FFN
    nf = D_FF // tf
    hb = tm // HALO
    const = lambda shape: pl.BlockSpec(shape, lambda bb, i, f: (0,) * len(shape))
    row = lambda width: pl.BlockSpec((1, tm, width), lambda bb, i, f: (bb, i, 0))
    halo = lambda width: pl.BlockSpec((1, HALO, width), lambda bb, i, f: (bb, jnp.maximum(i * hb - 1, 0), 0))
    na = am.shape[2]
    return pl.pallas_call(
        functools.partial(_ffn_kernel, final=final),
        grid=(b, s // tm, nf),
        in_specs=[row(d), halo(d), row(na), halo(na), row(na), halo(na),
                  const((na, d)), const((na, d)), const((1, d)),
                  pl.BlockSpec((d, tf), lambda bb, i, f: (0, f)),
                  pl.BlockSpec((d, tf), lambda bb, i, f: (0, nf + f)),
                  pl.BlockSpec((CONV_WIDTH, tf), lambda bb, i, f: (0, f)),
                  pl.BlockSpec((CONV_WIDTH, tf), lambda bb, i, f: (0, nf + f)),
                  pl.BlockSpec((1, tf), lambda bb, i, f: (0, f)),
                  pl.BlockSpec((1, tf), lambda bb, i, f: (0, nf + f)),
                  pl.BlockSpec((tf, d), lambda bb, i, f: (f, 0)),
                  const((1, d))],
        out_specs=row(d),
        out_shape=jax.ShapeDtypeStruct((b, s, d), F32),
        scratch_shapes=[pltpu.VMEM((tm, d), F32), pltpu.VMEM((HALO + tm, d), BF16), pltpu.VMEM((tm, d), F32)],
        compiler_params=pltpu.CompilerParams(
            dimension_semantics=("arbitrary", "arbitrary", "arbitrary"), vmem_limit_bytes=VMEM_LIMIT),
        name="out_ffn",
    )(x, x, am, am, an, an, w["wo_m"], w["wo_n"], w["ffn_norm"], w["w_up"], w["w_up"],
      w["conv_w"], w["conv_w"], w["conv_b"], w["conv_b"], w["w_down"], final_norm)


def _rope_tables(seq):
    def cs(dim):
        inv = ROPE_THETA ** (-jnp.arange(0, dim, 2, dtype=F32) / dim)
        ang = jnp.arange(seq, dtype=F32)[:, None] * inv[None, :]
        return jnp.cos(ang), jnp.sin(ang)

    def table(cos, sin, offsets):
        half = cos.shape[1]
        zero = jnp.zeros_like(sin)
        c, sa, sb, at = [], [], [], 0
        for o in offsets:
            gap = (seq, o - at)
            c += [jnp.ones(gap, F32), cos, cos]
            sa += [jnp.zeros(gap, F32), -sin, zero]
            sb += [jnp.zeros(gap, F32), zero, sin]
            at = o + 2 * half
        tail = (seq, LANES - at)
        cat = lambda parts, fill: jnp.concatenate(parts + [jnp.full(tail, fill, F32)], axis=1)
        return jnp.stack([cat(c, 1.0), cat(sa, 0.0), cat(sb, 0.0)])

    cm, sm = cs(MLA_ROPE_DIM)
    cn, sn = cs(NSA_ROT_DIM)
    return {"mla": table(cm, sm, (MLA_NOPE_DIM,)),
            "nq": table(cn, sn, (0, NSA_HEAD_DIM)),
            "nk": table(cn, sn, (0,))}


def _select_tables(seq):
    nc = seq // CMP_STRIDE
    n_c = (seq - CMP_LEN) // CMP_STRIDE + 1
    n_s = seq // SLC_LEN
    c = np.arange(nc)[:, None]
    sb = np.arange(LANES)[None, :]
    cs, ce = c * CMP_STRIDE, c * CMP_STRIDE + CMP_LEN - 1
    ss = sb * SLC_LEN
    ov = (cs <= ss + SLC_LEN - 1) & (ce >= ss) & (c < n_c) & (sb < n_s)
    ext = (np.arange(seq)[:, None] // SLC_LEN) == np.arange(LANES)[None, :]
    return jnp.asarray(ov.T, BF16), jnp.asarray(ext, BF16)


def _layer_weights(l, attn_norm, w_in, mla_q_norm, mla_kv_norm, mla_w_uq, mla_w_ukv, nsa_gate_bias,
                   nsa_cmp_pos, nsa_cmp_w1, nsa_cmp_b1, nsa_cmp_w2, w_out, ffn_norm, ffn_w_up,
                   ffn_conv_w, ffn_conv_b, ffn_w_down):
    d = D_MODEL
    wi = w_in[l]
    z = lambda n: jnp.zeros((d, n), F32)
    kpe = jnp.concatenate([z(MLA_NOPE_DIM), wi[:, _OFF[2]:_OFF[3]], z(LANES - MLA_QK_DIM)], axis=1)
    nkv = wi[:, _OFF[4]:_OFF[5]].reshape(d, NSA_BRANCHES, 2, NSA_KV_HEADS, NSA_HEAD_DIM)
    nkv = nkv.transpose(0, 1, 3, 2, 4).reshape(d, -1)
    gpg = NSA_GROUP * NSA_BRANCHES
    gw = wi[:, _OFF[5]:_OFF[6]].reshape(d, NSA_KV_HEADS, gpg)
    gw = jnp.pad(gw, ((0, 0), (0, 0), (0, LANES - gpg))).reshape(d, -1)
    w_in_r = jnp.concatenate([wi[:, _OFF[0]:_OFF[2]], kpe, wi[:, _OFF[3]:_OFF[4]], nkv, gw], axis=1)
    gb = jnp.pad(nsa_gate_bias[l].reshape(NSA_KV_HEADS, gpg), ((0, 0), (0, LANES - gpg))).reshape(1, -1)

    wuq = mla_w_uq[l].reshape(MLA_Q_RANK, MLA_HEADS, MLA_QK_DIM)
    wuq = jnp.pad(wuq, ((0, 0), (0, 0), (0, LANES - MLA_QK_DIM))).reshape(MLA_Q_RANK, -1)
    wukv = mla_w_ukv[l].reshape(MLA_KV_RANK, MLA_HEADS, 2, MLA_NOPE_DIM)
    wukv = jnp.pad(wukv, ((0, 0), (0, 0), (0, 0), (0, LANES - MLA_NOPE_DIM))).reshape(MLA_KV_RANK, -1)

    pos = jnp.concatenate([nsa_cmp_pos[l, 0], nsa_cmp_pos[l, 1]], axis=1)
    w1 = nsa_cmp_w1[l].reshape(2, CMP_LEN, NSA_HEAD_DIM, CMP_HIDDEN)
    zz = jnp.zeros_like(w1[0])
    w1c = jnp.concatenate([jnp.concatenate([w1[0], zz], axis=2), jnp.concatenate([zz, w1[1]], axis=2)], axis=1)
    b1c = nsa_cmp_b1[l].reshape(1, -1)
    w2 = nsa_cmp_w2[l]
    z2 = jnp.zeros_like(w2[0])
    w2c = jnp.concatenate([jnp.concatenate([w2[0], z2], axis=1), jnp.concatenate([z2, w2[1]], axis=1)], axis=0)

    wo = w_out[l]
    nm = MLA_HEADS * MLA_V_DIM
    wo_m = jnp.pad(wo[:nm].reshape(MLA_HEADS, MLA_V_DIM, d), ((0, 0), (0, LANES - MLA_V_DIM), (0, 0)))
    wo_n = jnp.pad(wo[nm:].reshape(NSA_HEADS, NSA_HEAD_DIM, d), ((0, 0), (0, LANES - NSA_HEAD_DIM), (0, 0)))
    return {
        "attn_norm": attn_norm[l].reshape(1, d), "w_in": w_in_r.astype(BF16),
        "q_norm": mla_q_norm[l].reshape(1, -1), "kv_norm": mla_kv_norm[l].reshape(1, -1),
        "w_uq": wuq.astype(BF16), "w_ukv": wukv.astype(BF16), "gate_bias": gb,
        "cmp_pos": pos, "cmp_w1": w1c.astype(BF16), "cmp_b1": b1c, "cmp_w2": w2c.astype(BF16),
        "wo_m": wo_m.reshape(-1, d).astype(BF16), "wo_n": wo_n.reshape(-1, d).astype(BF16),
        "ffn_norm": ffn_norm[l].reshape(1, d), "w_up": ffn_w_up[l].astype(BF16),
        "conv_w": ffn_conv_w[l], "conv_b": ffn_conv_b[l].reshape(1, -1), "w_down": ffn_w_down[l].astype(BF16),
    }


def kernel(x, attn_norm, w_in, mla_q_norm, mla_kv_norm, mla_w_uq, mla_w_ukv, nsa_gate_bias, nsa_cmp_pos, nsa_cmp_w1, nsa_cmp_b1, nsa_cmp_w2, w_out, ffn_norm, ffn_w_up, ffn_conv_w, ffn_conv_b, ffn_w_down, final_norm):
    b, s, d = x.shape
    depth = w_in.shape[0]
    assert d == D_MODEL and s % (CMP_STRIDE * LANES) == 0 and s // SLC_LEN <= LANES
    tabs = _rope_tables(s)
    ov, ext = _select_tables(s)
    fn = final_norm.reshape(1, d)
    for l in range(depth):
        w = _layer_weights(l, attn_norm, w_in, mla_q_norm, mla_kv_norm, mla_w_uq, mla_w_ukv, nsa_gate_bias,
                           nsa_cmp_pos, nsa_cmp_w1, nsa_cmp_b1, nsa_cmp_w2, w_out, ffn_norm, ffn_w_up,
                           ffn_conv_w, ffn_conv_b, ffn_w_down)
        mq, mkv, nq, nk, nv, ncmp, gates = _proj_call(x, w, tabs)
        kc, vc = _cmp_call(ncmp, w)
        o_mla = _mla_call(mq, mkv)
        o_nsa = _nsa_call(nq, kc, vc, nk, nv, gates, ov, ext)
        x = _ffn_call(x, o_mla, o_nsa, w, fn, final=(l == depth - 1))
    return x
```

```python
import functools

import jax
import jax.numpy as jnp
import numpy as np
from jax import lax
from jax.experimental import pallas as pl
from jax.experimental.pallas import tpu as pltpu

F32 = jnp.float32
BF16 = jnp.bfloat16

LANES = 128

D_MODEL = 1024
ROPE_THETA = 500000.0
NORM_EPS = 1e-6
NEG_INF = -1e30
LOG2E = 1.4426950408889634

MLA_HEADS = 8
MLA_Q_RANK = 256
MLA_KV_RANK = 128
MLA_NOPE_DIM = 64
MLA_ROPE_DIM = 32
MLA_V_DIM = 64
MLA_QK_DIM = MLA_NOPE_DIM + MLA_ROPE_DIM

NSA_HEADS = 8
NSA_KV_HEADS = 2
NSA_GROUP = NSA_HEADS // NSA_KV_HEADS
NSA_HEAD_DIM = 64
NSA_ROT_DIM = NSA_HEAD_DIM // 4
NSA_BRANCHES = 3
CMP_LEN = 32
CMP_STRIDE = 16
CMP_HIDDEN = 128
SLC_LEN = 64
SLC_TOPK = 16
WINDOW = 512
FORCE_SCORE = 1e6

D_FF = 2816
CONV_WIDTH = 3

IN_SPLITS = (MLA_Q_RANK, MLA_KV_RANK, MLA_ROPE_DIM, NSA_HEADS * NSA_HEAD_DIM,
             NSA_BRANCHES * 2 * NSA_KV_HEADS * NSA_HEAD_DIM, NSA_HEADS * NSA_BRANCHES)
_OFF = tuple(int(v) for v in np.cumsum((0,) + IN_SPLITS))

C_QLAT = 0
C_KVLAT = C_QLAT + MLA_Q_RANK
C_KPE = C_KVLAT + MLA_KV_RANK
C_NQ = C_KPE + LANES
C_NKV = C_NQ + NSA_HEADS * NSA_HEAD_DIM
C_GATE = C_NKV + NSA_BRANCHES * NSA_KV_HEADS * LANES
C_END = C_GATE + NSA_KV_HEADS * LANES

VMEM_LIMIT = 56 * 1024 * 1024

TM_PROJ = 512
TQ_MLA = 512
MLA_HEADS_PER_STEP = 4
TQ_NSA = 256
TK_SLC = 512
TM_FFN = 512
TF_FFN = 1408
HALO = 16


def _rms(x, g):
    return x * lax.rsqrt(jnp.mean(x * x, axis=-1, keepdims=True) + NORM_EPS) * g


def _rope(x, tab_ref, half):
    return (x * tab_ref[0] + pltpu.roll(x, LANES - half, 1) * tab_ref[1]
            + pltpu.roll(x, half, 1) * tab_ref[2])


def _dot(a, b):
    return jnp.dot(a, b, preferred_element_type=F32)


def _dot_nt(a, b):
    return lax.dot_general(a, b, (((1,), (1,)), ((), ())), preferred_element_type=F32)


def _proj_kernel(x_ref, g_ref, win_ref, qn_ref, kvn_ref, wuq_ref, wukv_ref, gb_ref,
                 tmla_ref, tnq_ref, tnk_ref,
                 mq_ref, mkv_ref, nq_ref, nk_ref, nv_ref, ncmp_ref, gate_ref):
    h = _rms(x_ref[0], g_ref[...])
    p = _dot(h.astype(BF16), win_ref[...])

    ql = _rms(p[:, C_QLAT:C_QLAT + MLA_Q_RANK], qn_ref[...])
    q = _dot(ql.astype(BF16), wuq_ref[...])
    scale = MLA_QK_DIM ** -0.5 * LOG2E
    for hh in range(MLA_HEADS):
        blk = q[:, hh * LANES:(hh + 1) * LANES]
        mq_ref[0, :, hh * LANES:(hh + 1) * LANES] = (
            _rope(blk, tmla_ref, MLA_ROPE_DIM // 2) * scale).astype(BF16)

    kpe = _rope(p[:, C_KPE:C_KPE + LANES], tmla_ref, MLA_ROPE_DIM // 2)
    kvl = _rms(p[:, C_KVLAT:C_KVLAT + MLA_KV_RANK], kvn_ref[...])
    kv = _dot(kvl.astype(BF16), wukv_ref[...])
    ones_hi = jnp.where(lax.broadcasted_iota(jnp.int32, kpe.shape, 1) >= MLA_V_DIM, 1.0, 0.0)
    for hh in range(MLA_HEADS):
        c0 = hh * 2 * LANES
        mkv_ref[0, :, c0:c0 + LANES] = (kv[:, c0:c0 + LANES] + kpe).astype(BF16)
        mkv_ref[0, :, c0 + LANES:c0 + 2 * LANES] = (kv[:, c0 + LANES:c0 + 2 * LANES] + ones_hi).astype(BF16)

    nscale = NSA_HEAD_DIM ** -0.5 * LOG2E
    for j in range(NSA_HEADS * NSA_HEAD_DIM // LANES):
        blk = p[:, C_NQ + j * LANES:C_NQ + (j + 1) * LANES]
        nq_ref[0, :, j * LANES:(j + 1) * LANES] = (
            _rope(blk, tnq_ref, NSA_ROT_DIM // 2) * nscale).astype(BF16)

    low = lax.broadcasted_iota(jnp.int32, kpe.shape, 1) < NSA_HEAD_DIM
    for j in range(NSA_BRANCHES * NSA_KV_HEADS):
        blk = _rope(p[:, C_NKV + j * LANES:C_NKV + (j + 1) * LANES], tnk_ref, NSA_ROT_DIM // 2)
        if j < NSA_KV_HEADS:
            ncmp_ref[0, :, j * LANES:(j + 1) * LANES] = blk
        else:
            jj = j - NSA_KV_HEADS
            nk_ref[0, :, jj * LANES:(jj + 1) * LANES] = jnp.where(low, blk, 0.0).astype(BF16)
            nv_ref[0, :, jj * LANES:(jj + 1) * LANES] = jnp.where(
                low, pltpu.roll(blk, NSA_HEAD_DIM, 1), 1.0).astype(BF16)

    gate_ref[0] = jax.nn.sigmoid(p[:, C_GATE:C_END] + gb_ref[...])


def _proj_call(x, w, tabs):
    b, s, d = x.shape
    tm = TM_PROJ
    const = lambda shape: pl.BlockSpec(shape, lambda i, bb: (0,) * len(shape))
    tab = pl.BlockSpec((3, tm, LANES), lambda i, bb: (0, i, 0))
    row = lambda width: pl.BlockSpec((1, tm, width), lambda i, bb: (bb, i, 0))
    return pl.pallas_call(
        _proj_kernel,
        grid=(s // tm, b),
        in_specs=[row(d), const((1, d)), const((d, C_END)), const((1, MLA_Q_RANK)),
                  const((1, MLA_KV_RANK)), const((MLA_Q_RANK, MLA_HEADS * LANES)),
                  const((MLA_KV_RANK, MLA_HEADS * 2 * LANES)), const((1, NSA_KV_HEADS * LANES)),
                  tab, tab, tab],
        out_specs=[row(MLA_HEADS * LANES), row(MLA_HEADS * 2 * LANES), row(NSA_HEADS * NSA_HEAD_DIM),
                   row(2 * NSA_KV_HEADS * LANES), row(2 * NSA_KV_HEADS * LANES),
                   row(NSA_KV_HEADS * LANES), row(NSA_KV_HEADS * LANES)],
        out_shape=[jax.ShapeDtypeStruct((b, s, MLA_HEADS * LANES), BF16),
                   jax.ShapeDtypeStruct((b, s, MLA_HEADS * 2 * LANES), BF16),
                   jax.ShapeDtypeStruct((b, s, NSA_HEADS * NSA_HEAD_DIM), BF16),
                   jax.ShapeDtypeStruct((b, s, 2 * NSA_KV_HEADS * LANES), BF16),
                   jax.ShapeDtypeStruct((b, s, 2 * NSA_KV_HEADS * LANES), BF16),
                   jax.ShapeDtypeStruct((b, s, NSA_KV_HEADS * LANES), F32),
                   jax.ShapeDtypeStruct((b, s, NSA_KV_HEADS * LANES), F32)],
        compiler_params=pltpu.CompilerParams(
            dimension_semantics=("arbitrary", "arbitrary"), vmem_limit_bytes=VMEM_LIMIT),
        name="proj",
    )(x, w["attn_norm"], w["w_in"], w["q_norm"], w["kv_norm"], w["w_uq"], w["w_ukv"], w["gate_bias"],
      tabs["mla"], tabs["nq"], tabs["nk"])


def _cmp_kernel(x_ref, pos_ref, w1_ref, b1_ref, w2_ref, kc_ref, vc_ref):
    nc = kc_ref.shape[2]
    half = CMP_LEN // 2
    a = jnp.zeros((nc, 2 * CMP_HIDDEN), F32)
    bm = jnp.zeros((nc, 2 * CMP_HIDDEN), F32)
    for l in range(half):
        xl = x_ref[0, pl.ds(l, nc, stride=CMP_STRIDE), :]
        a = a + _dot((xl + pos_ref[l:l + 1, :]).astype(BF16), w1_ref[l])
        bm = bm + _dot((xl + pos_ref[half + l:half + l + 1, :]).astype(BF16), w1_ref[half + l])
    hid = a + pltpu.roll(bm, nc - 1, 0) + b1_ref[...]
    r = _dot(jax.nn.gelu(hid).astype(BF16), w2_ref[...])
    low = lax.broadcasted_iota(jnp.int32, r.shape, 1) < NSA_HEAD_DIM
    kc_ref[0, 0] = jnp.where(low, r, 0.0).astype(BF16)
    vc_ref[0, 0] = jnp.where(low, pltpu.roll(r, NSA_HEAD_DIM, 1), 1.0).astype(BF16)


def _cmp_call(ncmp, w):
    b, s, _ = ncmp.shape
    nc = s // CMP_STRIDE
    const = lambda shape: pl.BlockSpec(shape, lambda bb, g: (0,) * len(shape))
    return pl.pallas_call(
        _cmp_kernel,
        grid=(b, NSA_KV_HEADS),
        in_specs=[pl.BlockSpec((1, s, LANES), lambda bb, g: (bb, 0, g)),
                  const((CMP_LEN, LANES)), const((CMP_LEN, LANES, 2 * CMP_HIDDEN)),
                  const((1, 2 * CMP_HIDDEN)), const((2 * CMP_HIDDEN, LANES))],
        out_specs=[pl.BlockSpec((1, 1, nc, LANES), lambda bb, g: (bb, g, 0, 0))] * 2,
        out_shape=[jax.ShapeDtypeStruct((b, NSA_KV_HEADS, nc, LANES), BF16)] * 2,
        compiler_params=pltpu.CompilerParams(
            dimension_semantics=("arbitrary", "arbitrary"), vmem_limit_bytes=VMEM_LIMIT),
        name="compress",
    )(ncmp, w["cmp_pos"], w["cmp_w1"], w["cmp_b1"], w["cmp_w2"])


def _mla_kernel(q_ref, kv_ref, o_ref, m_ref, acc_ref):
    tq = q_ref.shape[1]
    tk = tq
    qi = pl.program_id(2)
    m_ref[...] = jnp.full(m_ref.shape, NEG_INF, F32)
    acc_ref[...] = jnp.zeros(acc_ref.shape, F32)

    def step(j, masked):
        k0 = pl.multiple_of(j * tk, tk)
        for hh in range(MLA_HEADS_PER_STEP):
            cols = slice(hh * LANES, (hh + 1) * LANES)
            k = kv_ref[0, pl.ds(k0, tk), 2 * hh * LANES:(2 * hh + 1) * LANES]
            v = kv_ref[0, pl.ds(k0, tk), (2 * hh + 1) * LANES:(2 * hh + 2) * LANES]
            s = _dot_nt(q_ref[0, :, cols], k)
            if masked:
                s = jnp.where(lax.broadcasted_iota(jnp.int32, s.shape, 1)
                              <= lax.broadcasted_iota(jnp.int32, s.shape, 0), s, NEG_INF)
            m_old = m_ref[:, cols]
            m_new = jnp.maximum(m_old, jnp.max(s, axis=1, keepdims=True))
            p = jnp.exp2((s - jnp.tile(m_new, (1, tk // LANES))).astype(BF16))
            acc_ref[:, cols] = jnp.exp2(m_old - m_new) * acc_ref[:, cols] + _dot(p, v)
            m_ref[:, cols] = m_new

    def body(jj, c):
        step(2 * jj, False)
        step(2 * jj + 1, False)
        return c

    lax.fori_loop(0, qi // 2, body, 0)

    @pl.when(qi % 2 == 1)
    def _():
        step(qi - 1, False)

    step(qi, True)
    for hh in range(MLA_HEADS_PER_STEP):
        cols = slice(hh * LANES, (hh + 1) * LANES)
        acc = acc_ref[:, cols]
        lane = lax.broadcasted_iota(jnp.int32, acc.shape, 1)
        denom = jnp.where(lane < MLA_V_DIM, pltpu.roll(acc, MLA_V_DIM, 1), acc)
        o_ref[0, :, cols] = (acc / denom).astype(BF16)


def _mla_call(mq, mkv):
    b, s, _ = mq.shape
    tq = TQ_MLA
    hp = MLA_HEADS_PER_STEP
    return pl.pallas_call(
        _mla_kernel,
        grid=(b, MLA_HEADS // hp, s // tq),
        in_specs=[pl.BlockSpec((1, tq, hp * LANES), lambda bb, h, i: (bb, i, h)),
                  pl.BlockSpec((1, s, 2 * hp * LANES), lambda bb, h, i: (bb, 0, h))],
        out_specs=pl.BlockSpec((1, tq, hp * LANES), lambda bb, h, i: (bb, i, h)),
        out_shape=jax.ShapeDtypeStruct((b, s, MLA_HEADS * LANES), BF16),
        scratch_shapes=[pltpu.VMEM((tq, hp * LANES), F32), pltpu.VMEM((tq, hp * LANES), F32)],
        compiler_params=pltpu.CompilerParams(
            dimension_semantics=("arbitrary", "arbitrary", "arbitrary"), vmem_limit_bytes=VMEM_LIMIT),
        name="mla_attn",
    )(mq, mkv)


def _nsa_kernel(q_ref, kc_ref, vc_ref, ks_ref, vs_ref, kw_ref, vw_ref, gate_ref, ovt_ref, ext_ref, o_ref,
                m_ref, acc_ref, *, seq):
    tq = q_ref.shape[1]
    hg = NSA_GROUP
    nc = kc_ref.shape[2]
    tk = TK_SLC
    q0 = pl.program_id(2) * tq

    xq = q_ref[0].astype(F32)
    parts = []
    for j in range(hg * NSA_HEAD_DIM // LANES):
        a = xq[:, j * LANES:(j + 1) * LANES]
        parts.append(a)
        parts.append(pltpu.roll(a, NSA_HEAD_DIM, 1))
    qst = jnp.concatenate(parts, axis=0).astype(BF16)

    def normalized(acc):
        lane = lax.broadcasted_iota(jnp.int32, acc.shape, 1)
        return acc / jnp.where(lane < NSA_HEAD_DIM, pltpu.roll(acc, NSA_HEAD_DIM, 1), acc)

    kc = kc_ref[0, 0]
    vc = vc_ref[0, 0]
    sc = _dot_nt(qst, kc)
    qpos_c = q0 + lax.broadcasted_iota(jnp.int32, (tq, nc), 0)
    cend = lax.broadcasted_iota(jnp.int32, (tq, nc), 1) * CMP_STRIDE + (CMP_LEN - 1)
    mask_c = cend <= qpos_c
    pcs = jnp.zeros((tq, nc), F32)
    o_cmp = []
    lane = lax.broadcasted_iota(jnp.int32, (tq, LANES), 1)
    sees_any = q0 + lax.broadcasted_iota(jnp.int32, (tq, LANES), 0) >= CMP_LEN - 1
    for h in range(hg):
        sh = jnp.where(mask_c, sc[h * tq:(h + 1) * tq], NEG_INF)
        e = jnp.exp2(sh - jnp.max(sh, axis=1, keepdims=True))
        acc = _dot(e.astype(BF16), vc)
        inv = jnp.where(sees_any, 1.0 / jnp.where(lane < NSA_HEAD_DIM, pltpu.roll(acc, NSA_HEAD_DIM, 1), acc), 0.0)
        pcs = pcs + e * jnp.tile(inv, (1, nc // LANES))
        o_cmp.append(acc * inv)
    ovt = ovt_ref[...]
    p_hi = pcs.astype(BF16)
    r1 = pcs - p_hi.astype(F32)
    p_mid = r1.astype(BF16)
    p_lo = (r1 - p_mid.astype(F32)).astype(BF16)
    imp = _dot_nt(ovt, p_hi) + _dot_nt(ovt, p_mid) + _dot_nt(ovt, p_lo)

    blk = lax.broadcasted_iota(jnp.int32, (LANES, tq), 0)
    cur = (q0 + lax.broadcasted_iota(jnp.int32, (LANES, tq), 1)) // SLC_LEN
    forced = (blk == 0) | (blk == cur) | (blk == cur - 1)
    allowed = blk <= cur
    wl = WINDOW + tq
    start = pl.multiple_of(jnp.maximum(q0 - WINDOW, 0), tq)
    sw = _dot_nt(qst, kw_ref[0, pl.ds(start, wl), :])
    vw = vw_ref[0, pl.ds(start, wl), :]
    qpos_w = q0 + lax.broadcasted_iota(jnp.int32, (tq, wl), 0)
    kpos_w = start + lax.broadcasted_iota(jnp.int32, (tq, wl), 1)
    mask_w = jnp.where(kpos_w <= qpos_w, kpos_w, -WINDOW - seq) > qpos_w - WINDOW
    o_win = []
    for h in range(hg):
        sh = jnp.where(mask_w, sw[h * tq:(h + 1) * tq], NEG_INF)
        e = jnp.exp2(sh - jnp.max(sh, axis=1, keepdims=True))
        o_win.append(normalized(_dot(e.astype(BF16), vw)))

    score = jnp.where(forced, -2.0, jnp.where(allowed, imp, -1.0))
    blk_f = blk.astype(F32)
    bias = jnp.where(forced, 0.0, NEG_INF)
    for _ in range(SLC_TOPK - 3):
        mx = jnp.max(score, axis=0, keepdims=True)
        first = jnp.min(jnp.where(score == mx, blk_f, float(LANES)), axis=0, keepdims=True)
        hit = blk_f == first
        bias = jnp.where(hit, 0.0, bias)
        score = jnp.where(hit, -2.0, score)
    selneg = jnp.where(allowed, bias, NEG_INF).T.astype(BF16)
    qaug = jnp.concatenate([qst, jnp.concatenate([selneg] * hg, axis=0)], axis=1)

    m_ref[...] = jnp.full(m_ref.shape, NEG_INF, F32)
    acc_ref[...] = jnp.zeros(acc_ref.shape, F32)

    def slc_step(j, masked):
        k0 = pl.multiple_of(j * tk, tk)
        kaug = jnp.concatenate([ks_ref[0, pl.ds(k0, tk), :], ext_ref[pl.ds(k0, tk), :]], axis=1)
        s = _dot_nt(qaug, kaug)
        v = vs_ref[0, pl.ds(k0, tk), :]
        if masked:
            causal = (k0 + lax.broadcasted_iota(jnp.int32, (tq, tk), 1)
                      <= q0 + lax.broadcasted_iota(jnp.int32, (tq, tk), 0))
        for h in range(hg):
            rows = slice(h * tq, (h + 1) * tq)
            sh = s[rows]
            if masked:
                sh = jnp.where(causal, sh, NEG_INF)
            m_old = m_ref[rows]
            m_new = jnp.maximum(m_old, jnp.max(sh, axis=1, keepdims=True))
            p = jnp.exp2((sh - jnp.tile(m_new, (1, tk // LANES))).astype(BF16))
            acc_ref[rows] = jnp.exp2(m_old - m_new) * acc_ref[rows] + _dot(p, v)
            m_ref[rows] = m_new

    def slc_body(jj, c):
        slc_step(2 * jj, False)
        slc_step(2 * jj + 1, False)
        return c

    n_full = q0 // tk
    lax.fori_loop(0, n_full // 2, slc_body, 0)

    @pl.when(n_full % 2 == 1)
    def _():
        slc_step(n_full - 1, False)

    slc_step(n_full, True)

    gates = gate_ref[0]
    for h in range(hg):
        o_slc = normalized(acc_ref[h * tq:(h + 1) * tq])
        c = h * NSA_BRANCHES
        o = gates[:, c:c + 1] * o_cmp[h] + gates[:, c + 1:c + 2] * o_slc + gates[:, c + 2:c + 3] * o_win[h]
        o_ref[0, :, h * LANES:(h + 1) * LANES] = o.astype(BF16)


def _nsa_call(nq, kc, vc, nk, nv, gates, ov, ext):
    b, s, _ = nq.shape
    tq = TQ_NSA
    hg = NSA_GROUP
    nc = kc.shape[2]
    cmp_spec = pl.BlockSpec((1, 1, nc, LANES), lambda bb, g, i: (bb, g, 0, 0))
    slc_spec = pl.BlockSpec((1, s, LANES), lambda bb, g, i: (bb, 0, g))
    win_spec = pl.BlockSpec((1, s, LANES), lambda bb, g, i: (bb, 0, NSA_KV_HEADS + g))
    return pl.pallas_call(
        functools.partial(_nsa_kernel, seq=s),
        grid=(b, NSA_KV_HEADS, s // tq),
        in_specs=[pl.BlockSpec((1, tq, hg * NSA_HEAD_DIM), lambda bb, g, i: (bb, i, g)),
                  cmp_spec, cmp_spec, slc_spec, slc_spec, win_spec, win_spec,
                  pl.BlockSpec((1, tq, LANES), lambda bb, g, i: (bb, i, g)),
                  pl.BlockSpec((LANES, nc), lambda bb, g, i: (0, 0)),
                  pl.BlockSpec((s, LANES), lambda bb, g, i: (0, 0))],
        out_specs=pl.BlockSpec((1, tq, hg * LANES), lambda bb, g, i: (bb, i, g)),
        out_shape=jax.ShapeDtypeStruct((b, s, NSA_HEADS * LANES), BF16),
        scratch_shapes=[pltpu.VMEM((hg * tq, LANES), F32), pltpu.VMEM((hg * tq, LANES), F32)],
        compiler_params=pltpu.CompilerParams(
            dimension_semantics=("arbitrary", "arbitrary", "arbitrary"), vmem_limit_bytes=VMEM_LIMIT),
        name="nsa_attn",
    )(nq, kc, vc, nk, nv, nk, nv, gates, ov, ext)


def _ffn_kernel(x_ref, xh_ref, am_ref, amh_ref, an_ref, anh_ref, wom_ref, won_ref, g_ref,
                wu_ref, wv_ref, cwu_ref, cwv_ref, cbu_ref, cbv_ref, wd_ref, fn_ref, o_ref,
                x1_ref, hn_ref, acc_ref, *, final):
    tm = x_ref.shape[1]
    i = pl.program_id(1)
    f = pl.program_id(2)

    @pl.when(f == 0)
    def _():
        def piece(xv, am, an):
            x1 = xv + _dot(am, wom_ref[...]) + _dot(an, won_ref[...])
            return x1, _rms(x1, g_ref[...]).astype(BF16)

        x1, hn = piece(x_ref[0], am_ref[0], an_ref[0])
        x1_ref[...] = x1
        hn_ref[HALO:, :] = hn
        _, hnh = piece(xh_ref[0], amh_ref[0], anh_ref[0])
        hn_ref[0:HALO, :] = jnp.where(i > 0, hnh, jnp.zeros_like(hnh))
        acc_ref[...] = jnp.zeros(acc_ref.shape, F32)

    hn = hn_ref[...]

    def conv(w_ref, cw_ref, cb_ref):
        hh = _dot(hn, w_ref[...])
        return (hh[HALO:] * cw_ref[2:3, :] + hh[HALO - 1:HALO - 1 + tm] * cw_ref[1:2, :]
                + hh[HALO - 2:HALO - 2 + tm] * cw_ref[0:1, :] + cb_ref[...])

    u = conv(wu_ref, cwu_ref, cbu_ref)
    v = conv(wv_ref, cwv_ref, cbv_ref)
    acc_ref[...] += _dot((u * jax.nn.sigmoid(u) * v).astype(BF16), wd_ref[...])

    @pl.when(f == pl.num_programs(2) - 1)
    def _():
        y = x1_ref[...] + acc_ref[...]
        if final:
            y = _rms(y, fn_ref[...])
        o_ref[0] = y


def _ffn_call(x, am, an, w, final_norm, final):
    b, s, d = x.shape
    tm, tf = TM_FFN, TF_FFN
    nf = D_FF // tf
    hb = tm // HALO
    const = lambda shape: pl.BlockSpec(shape, lambda bb, i, f: (0,) * len(shape))
    row = lambda width: pl.BlockSpec((1, tm, width), lambda bb, i, f: (bb, i, 0))
    halo = lambda width: pl.BlockSpec((1, HALO, width), lambda bb, i, f: (bb, jnp.maximum(i * hb - 1, 0), 0))
    na = am.shape[2]
    return pl.pallas_call(
        functools.partial(_ffn_kernel, final=final),
        grid=(b, s // tm, nf),
        in_specs=[row(d), halo(d), row(na), halo(na), row(na), halo(na),
                  const((na, d)), const((na, d)), const((1, d)),
                  pl.BlockSpec((d, tf), lambda bb, i, f: (0, f)),
                  pl.BlockSpec((d, tf), lambda bb, i, f: (0, nf + f)),
                  pl.BlockSpec((CONV_WIDTH, tf), lambda bb, i, f: (0, f)),
                  pl.BlockSpec((CONV_WIDTH, tf), lambda bb, i, f: (0, nf + f)),
                  pl.BlockSpec((1, tf), lambda bb, i, f: (0, f)),
                  pl.BlockSpec((1, tf), lambda bb, i, f: (0, nf + f)),
                  pl.BlockSpec((tf, d), lambda bb, i, f: (f, 0)),
                  const((1, d))],
        out_specs=row(d),
        out_shape=jax.ShapeDtypeStruct((b, s, d), F32),
        scratch_shapes=[pltpu.VMEM((tm, d), F32), pltpu.VMEM((HALO + tm, d), BF16), pltpu.VMEM((tm, d), F32)],
        compiler_params=pltpu.CompilerParams(
            dimension_semantics=("arbitrary", "arbitrary", "arbitrary"), vmem_limit_bytes=VMEM_LIMIT),
        name="out_ffn",
    )(x, x, am, am, an, an, w["wo_m"], w["wo_n"], w["ffn_norm"], w["w_up"], w["w_up"],
      w["conv_w"], w["conv_w"], w["conv_b"], w["conv_b"], w["w_down"], final_norm)


def _rope_tables(seq):
    def cs(dim):
        inv = ROPE_THETA ** (-jnp.arange(0, dim, 2, dtype=F32) / dim)
        ang = jnp.arange(seq, dtype=F32)[:, None] * inv[None, :]
        return jnp.cos(ang), jnp.sin(ang)

    def table(cos, sin, offsets):
        half = cos.shape[1]
        zero = jnp.zeros_like(sin)
        c, sa, sb, at = [], [], [], 0
        for o in offsets:
            gap = (seq, o - at)
            c += [jnp.ones(gap, F32), cos, cos]
            sa += [jnp.zeros(gap, F32), -sin, zero]
            sb += [jnp.zeros(gap, F32), zero, sin]
            at = o + 2 * half
        tail = (seq, LANES - at)
        cat = lambda parts, fill: jnp.concatenate(parts + [jnp.full(tail, fill, F32)], axis=1)
        return jnp.stack([cat(c, 1.0), cat(sa, 0.0), cat(sb, 0.0)])

    cm, sm = cs(MLA_ROPE_DIM)
    cn, sn = cs(NSA_ROT_DIM)
    return {"mla": table(cm, sm, (MLA_NOPE_DIM,)),
            "nq": table(cn, sn, (0, NSA_HEAD_DIM)),
            "nk": table(cn, sn, (0,))}


def _select_tables(seq):
    nc = seq // CMP_STRIDE
    n_c = (seq - CMP_LEN) // CMP_STRIDE + 1
    n_s = seq // SLC_LEN
    c = np.arange(nc)[:, None]
    sb = np.arange(LANES)[None, :]
    cs, ce = c * CMP_STRIDE, c * CMP_STRIDE + CMP_LEN - 1
    ss = sb * SLC_LEN
    ov = (cs <= ss + SLC_LEN - 1) & (ce >= ss) & (c < n_c) & (sb < n_s)
    ext = (np.arange(seq)[:, None] // SLC_LEN) == np.arange(LANES)[None, :]
    return jnp.asarray(ov.T, BF16), jnp.asarray(ext, BF16)


def _layer_weights(l, attn_norm, w_in, mla_q_norm, mla_kv_norm, mla_w_uq, mla_w_ukv, nsa_gate_bias,
                   nsa_cmp_pos, nsa_cmp_w1, nsa_cmp_b1, nsa_cmp_w2, w_out, ffn_norm, ffn_w_up,
                   ffn_conv_w, ffn_conv_b, ffn_w_down):
    d = D_MODEL
    wi = w_in[l]
    z = lambda n: jnp.zeros((d, n), F32)
    kpe = jnp.concatenate([z(MLA_NOPE_DIM), wi[:, _OFF[2]:_OFF[3]], z(LANES - MLA_QK_DIM)], axis=1)
    nkv = wi[:, _OFF[4]:_OFF[5]].reshape(d, NSA_BRANCHES, 2, NSA_KV_HEADS, NSA_HEAD_DIM)
    nkv = nkv.transpose(0, 1, 3, 2, 4).reshape(d, -1)
    gpg = NSA_GROUP * NSA_BRANCHES
    gw = wi[:, _OFF[5]:_OFF[6]].reshape(d, NSA_KV_HEADS, gpg)
    gw = jnp.pad(gw, ((0, 0), (0, 0), (0, LANES - gpg))).reshape(d, -1)
    w_in_r = jnp.concatenate([wi[:, _OFF[0]:_OFF[2]], kpe, wi[:, _OFF[3]:_OFF[4]], nkv, gw], axis=1)
    gb = jnp.pad(nsa_gate_bias[l].reshape(NSA_KV_HEADS, gpg), ((0, 0), (0, LANES - gpg))).reshape(1, -1)

    wuq = mla_w_uq[l].reshape(MLA_Q_RANK, MLA_HEADS, MLA_QK_DIM)
    wuq = jnp.pad(wuq, ((0, 0), (0, 0), (0, LANES - MLA_QK_DIM))).reshape(MLA_Q_RANK, -1)
    wukv = mla_w_ukv[l].reshape(MLA_KV_RANK, MLA_HEADS, 2, MLA_NOPE_DIM)
    wukv = jnp.pad(wukv, ((0, 0), (0, 0), (0, 0), (0, LANES - MLA_NOPE_DIM))).reshape(MLA_KV_RANK, -1)

    pos = jnp.concatenate([nsa_cmp_pos[l, 0], nsa_cmp_pos[l, 1]], axis=1)
    w1 = nsa_cmp_w1[l].reshape(2, CMP_LEN, NSA_HEAD_DIM, CMP_HIDDEN)
    zz = jnp.zeros_like(w1[0])
    w1c = jnp.concatenate([jnp.concatenate([w1[0], zz], axis=2), jnp.concatenate([zz, w1[1]], axis=2)], axis=1)
    b1c = nsa_cmp_b1[l].reshape(1, -1)
    w2 = nsa_cmp_w2[l]
    z2 = jnp.zeros_like(w2[0])
    w2c = jnp.concatenate([jnp.concatenate([w2[0], z2], axis=1), jnp.concatenate([z2, w2[1]], axis=1)], axis=0)

    wo = w_out[l]
    nm = MLA_HEADS * MLA_V_DIM
    wo_m = jnp.pad(wo[:nm].reshape(MLA_HEADS, MLA_V_DIM, d), ((0, 0), (0, LANES - MLA_V_DIM), (0, 0)))
    wo_n = jnp.pad(wo[nm:].reshape(NSA_HEADS, NSA_HEAD_DIM, d), ((0, 0), (0, LANES - NSA_HEAD_DIM), (0, 0)))
    return {
        "attn_norm": attn_norm[l].reshape(1, d), "w_in": w_in_r.astype(BF16),
        "q_norm": mla_q_norm[l].reshape(1, -1), "kv_norm": mla_kv_norm[l].reshape(1, -1),
        "w_uq": wuq.astype(BF16), "w_ukv": wukv.astype(BF16), "gate_bias": gb,
        "cmp_pos": pos, "cmp_w1": w1c.astype(BF16), "cmp_b1": b1c, "cmp_w2": w2c.astype(BF16),
        "wo_m": wo_m.reshape(-1, d).astype(BF16), "wo_n": wo_n.reshape(-1, d).astype(BF16),
        "ffn_norm": ffn_norm[l].reshape(1, d), "w_up": ffn_w_up[l].astype(BF16),
        "conv_w": ffn_conv_w[l], "conv_b": ffn_conv_b[l].reshape(1, -1), "w_down": ffn_w_down[l].astype(BF16),
    }


def kernel(x, attn_norm, w_in, mla_q_norm, mla_kv_norm, mla_w_uq, mla_w_ukv, nsa_gate_bias, nsa_cmp_pos, nsa_cmp_w1, nsa_cmp_b1, nsa_cmp_w2, w_out, ffn_norm, ffn_w_up, ffn_conv_w, ffn_conv_b, ffn_w_down, final_norm):
    b, s, d = x.shape
    depth = w_in.shape[0]
    assert d == D_MODEL and s % (CMP_STRIDE * LANES) == 0 and s // SLC_LEN <= LANES
    tabs = _rope_tables(s)
    ov, ext = _select_tables(s)
    fn = final_norm.reshape(1, d)
    for l in range(depth):
        w = _layer_weights(l, attn_norm, w_in, mla_q_norm, mla_kv_norm, mla_w_uq, mla_w_ukv, nsa_gate_bias,
                           nsa_cmp_pos, nsa_cmp_w1, nsa_cmp_b1, nsa_cmp_w2, w_out, ffn_norm, ffn_w_up,
                           ffn_conv_w, ffn_conv_b, ffn_w_down)
        mq, mkv, nq, nk, nv, ncmp, gates = _proj_call(x, w, tabs)
        kc, vc = _cmp_call(ncmp, w)
        o_mla = _mla_call(mq, mkv)
        o_nsa = _nsa_call(nq, kc, vc, nk, nv, gates, ov, ext)
        x = _ffn_call(x, o_mla, o_nsa, w, fn, final=(l == depth - 1))
    return x
```
